```python
import jax, jax.numpy as jnp
from jax import lax
import numpy as np

D_MODEL = 1024
BATCH = 2
SEQ = 8192
DEPTH = 2

CTX_LEN = 256
GRID_W = 64
ROPE_THETA = 10000.0
NORM_EPS = 1e-6

RET_HEADS = 4
RET_DK = 32
RET_DV = 64
RET_CHUNK = 128
RET_WIDTH = RET_HEADS * RET_DV
MLA_HEADS = 8
MLA_Q_RANK = 384
MLA_KV_RANK = 256
MLA_NOPE = 64
MLA_ROPE = 32
MLA_DV = 64
MLA_SCALE = (MLA_NOPE + MLA_ROPE) ** -0.5
ATTN_BLOCK = 128
MLA_WIDTH = MLA_HEADS * MLA_DV
POOL_WINDOWS = (2, 4, 8, 16)
POOL_GROUP = 64
POOL_WIDTH = 4 * POOL_GROUP

MIX_WIDTH = RET_WIDTH + MLA_WIDTH + POOL_WIDTH
IN_SECTIONS = (RET_HEADS * RET_DK, RET_HEADS * RET_DK, RET_WIDTH, RET_WIDTH,
               MLA_Q_RANK, MLA_KV_RANK, MLA_ROPE, POOL_WIDTH)
IN_WIDTH = 2 * RET_HEADS * RET_DK + 2 * RET_WIDTH + MLA_Q_RANK + MLA_KV_RANK + MLA_ROPE + POOL_WIDTH

D_FF = 2816
CONV_WIDTH = 3

kernel_name = 'hybrid_ret_mla_pool_prefix_dit'


def rms_norm(x, g):
    xf = x.astype(jnp.float32)
    y = xf * lax.rsqrt(jnp.mean(xf * xf, axis=-1, keepdims=True) + NORM_EPS)
    return (y * g.astype(jnp.float32)).astype(x.dtype)


def modulate(h, shift, scale):
    return h * (1.0 + scale) + shift


def split_in_proj(p):
    offs = []
    acc = 0
    for s in IN_SECTIONS[:-1]:
        acc += s
        offs.append(acc)
    return jnp.split(p, offs, axis=-1)


def split_heads(t, n_heads):
    B, T, _ = t.shape
    return t.reshape(B, T, n_heads, -1).transpose(0, 2, 1, 3)


def merge_heads(t):
    B, H, T, d = t.shape
    return t.transpose(0, 2, 1, 3).reshape(B, T, H * d)


def axial_rope_tables(rows, rot_dim):
    row = jnp.repeat(jnp.arange(rows, dtype=jnp.float32), GRID_W)
    col = jnp.tile(jnp.arange(GRID_W, dtype=jnp.float32), rows)
    n_freq = rot_dim // 4
    inv = ROPE_THETA ** (-jnp.arange(n_freq, dtype=jnp.float32) / n_freq)
    ang = jnp.concatenate([row[:, None] * inv, col[:, None] * inv], axis=-1)
    return jnp.cos(ang), jnp.sin(ang)


def apply_rope(x, cos, sin):
    xf = x.astype(jnp.float32)
    h = xf.shape[-1] // 2
    x1, x2 = xf[..., :h], xf[..., h:]
    return jnp.concatenate([x1 * cos - x2 * sin, x2 * cos + x1 * sin], axis=-1).astype(x.dtype)


def retention_chunkwise(q, k, v, log_g, s0, include_diag):
    B, H, T, _ = q.shape
    dv = v.shape[-1]
    n = T // RET_CHUNK
    idx = jnp.arange(RET_CHUNK, dtype=jnp.float32)
    diff = idx[:, None] - idx[None, :]
    mask = diff >= (0.0 if include_diag else 1.0)
    dmat = jnp.where(mask, jnp.exp(log_g[:, None, None] * jnp.maximum(diff, 0.0)), 0.0)
    q_dec = jnp.exp(log_g[:, None] * (idx + 1.0))[:, :, None]
    k_dec = jnp.exp(log_g[:, None] * (RET_CHUNK - 1.0 - idx))[:, :, None]
    c_dec = jnp.exp(log_g * RET_CHUNK)[:, None, None]

    def chunks(t):
        return jnp.moveaxis(t.reshape(B, H, n, RET_CHUNK, t.shape[-1]), 2, 0)

    def step(s, inp):
        qi, ki, vi = inp
        a = jnp.einsum('bhqd,bhkd->bhqk', qi, ki) * dmat
        o = jnp.einsum('bhqk,bhkv->bhqv', a, vi) + jnp.einsum('bhqd,bhdv->bhqv', qi * q_dec, s)
        s = s * c_dec + jnp.einsum('bhkd,bhkv->bhdv', ki * k_dec, vi)
        return s, o

    _, o = lax.scan(step, s0, (chunks(q), chunks(k), chunks(v)))
    return jnp.moveaxis(o, 0, 2).reshape(B, H, T, dv)


def retention_mixer(lat, ctx, decay_f, decay_b, cos, sin, with_ctx_out):
    log_gf = jax.nn.log_sigmoid(decay_f.astype(jnp.float32))
    log_gb = jax.nn.log_sigmoid(decay_b.astype(jnp.float32))

    def ret_q(q, rotate):
        q = split_heads(q, RET_HEADS).astype(jnp.float32)
        return apply_rope(q, cos, sin) if rotate else q

    def ret_kv(k, v, rotate):
        k = split_heads(k, RET_HEADS).astype(jnp.float32) * (RET_DK ** -0.5)
        if rotate:
            k = apply_rope(k, cos, sin)
        return k, split_heads(v, RET_HEADS).astype(jnp.float32)

    def bidir(q, k, v, s_f, s_b):
        fwd = retention_chunkwise(q, k, v, log_gf, s_f, True)
        bwd = retention_chunkwise(jnp.flip(q, 2), jnp.flip(k, 2), jnp.flip(v, 2), log_gb, s_b, False)
        return fwd + jnp.flip(bwd, 2)

    def finish(o, g):
        o = o * lax.rsqrt(jnp.mean(o * o, axis=-1, keepdims=True) + NORM_EPS)
        return (merge_heads(o) * jax.nn.silu(g.astype(jnp.float32))).astype(g.dtype)

    kc, vc = ret_kv(ctx[1], ctx[2], False)
    tc = kc.shape[2]
    m = jnp.arange(tc, dtype=jnp.float32)
    s_f = jnp.einsum('bhmd,bhme->bhde', kc * jnp.exp(log_gf[:, None] * (tc - 1.0 - m))[:, :, None], vc)
    s_b = jnp.einsum('bhmd,bhme->bhde', kc * jnp.exp(log_gb[:, None] * m)[:, :, None], vc)

    ql = ret_q(lat[0], True)
    kl, vl = ret_kv(lat[1], lat[2], True)
    y_lat = finish(bidir(ql, kl, vl, s_f, s_b), lat[3])
    y_ctx = None
    if with_ctx_out:
        z = jnp.zeros_like(s_f)
        y_ctx = finish(bidir(ret_q(ctx[0], False), kc, vc, z, z), ctx[3])
    return y_lat, y_ctx


def mla_attend(qn, qr, kn, kr, v):
    s = jnp.einsum('bhqd,bhkd->bhqk', qn, kn) + jnp.einsum('bhqr,bkr->bhqk', qr, kr)
    p = jax.nn.softmax(s.astype(jnp.float32) * MLA_SCALE, axis=-1)
    return jnp.einsum('bhqk,bhkd->bhqd', p.astype(v.dtype), v)


def mla_mixer(lat, ctx, q_norm_g, w_uq, kv_norm_g, w_ukv, cos, sin, with_ctx_out):
    def queries(cq, rotate):
        B, T, _ = cq.shape
        q = (rms_norm(cq, q_norm_g) @ w_uq).reshape(B, T, MLA_HEADS, MLA_NOPE + MLA_ROPE).transpose(0, 2, 1, 3)
        qn, qr = q[..., :MLA_NOPE], q[..., MLA_NOPE:]
        return qn, (apply_rope(qr, cos, sin) if rotate else qr)

    def keys_values(ckv, kr, rotate):
        B, T, _ = ckv.shape
        kv = (rms_norm(ckv, kv_norm_g) @ w_ukv).reshape(B, T, MLA_HEADS, MLA_NOPE + MLA_DV).transpose(0, 2, 1, 3)
        kr = apply_rope(kr, cos, sin) if rotate else kr
        return kv[..., :MLA_NOPE], kr, kv[..., MLA_NOPE:]

    kn_c, kr_c, v_c = keys_values(ctx[1], ctx[2], False)
    kn_l, kr_l, v_l = keys_values(lat[1], lat[2], True)
    kn = jnp.concatenate([kn_c, kn_l], axis=2)
    kr = jnp.concatenate([kr_c, kr_l], axis=1)
    v = jnp.concatenate([v_c, v_l], axis=2)

    qn, qr = queries(lat[0], True)
    B, H, T, _ = qn.shape
    nb = T // ATTN_BLOCK

    def blocks(t):
        return jnp.moveaxis(t.reshape(B, H, nb, ATTN_BLOCK, t.shape[-1]), 2, 0)

    o = lax.map(lambda qb: mla_attend(qb[0], qb[1], kn, kr, v), (blocks(qn), blocks(qr)))
    y_lat = merge_heads(jnp.moveaxis(o, 0, 2).reshape(B, H, T, MLA_DV))
    y_ctx = None
    if with_ctx_out:
        qn_c, qr_c = queries(ctx[0], False)
        y_ctx = merge_heads(mla_attend(qn_c, qr_c, kn_c, kr_c, v_c))
    return y_lat, y_ctx


def pool_mixer(x, pool_w, pool_scale):
    B, T, _ = x.shape
    xf = x.astype(jnp.float32)
    cs = jnp.pad(jnp.cumsum(xf, axis=1), ((0, 0), (1, 0), (0, 0)))
    t = jnp.arange(T)
    parts = []
    for gi, w in enumerate(POOL_WINDOWS):
        sl = slice(gi * POOL_GROUP, (gi + 1) * POOL_GROUP)
        lo = jnp.clip(t - w // 2, 0, T)
        hi = jnp.clip(t + w - w // 2, 0, T)
        csg = cs[:, :, sl]
        mean = (csg[:, hi] - csg[:, lo]) / (hi - lo).astype(jnp.float32)[None, :, None]
        parts.append(mean - xf[:, :, sl])
    d = jnp.stack(parts, axis=2)
    y = jnp.einsum('btgc,gcd->btgd', d, pool_w.astype(jnp.float32)).reshape(B, T, POOL_WIDTH)
    return (y * pool_scale.astype(jnp.float32)).astype(x.dtype)


def conv_ffn(h, w_up, conv_w, conv_b, w_down):
    u = h @ w_up
    up = jnp.pad(u, ((0, 0), (1, 1), (0, 0)))
    u = up[:, :-2] * conv_w[0] + up[:, 1:-1] * conv_w[1] + up[:, 2:] * conv_w[2] + conv_b
    a, b = jnp.split(u, 2, axis=-1)
    return (jax.nn.silu(a) * b) @ w_down


def trunk_layer(x, xc, mod, mod_c, norm1_g, w_in, ret_decay_f, ret_decay_b, mla_q_norm_g, w_uq,
                mla_kv_norm_g, w_ukv, pool_w, pool_scale, w_out, norm2_g, w_up, conv_w, conv_b, w_down,
                cos_r, sin_r, cos_m, sin_m, with_ctx_out):
    sh1, sc1, g1, sh2, sc2, g2 = jnp.split(mod, 6, axis=-1)
    sh1c, sc1c, g1c, sh2c, sc2c, g2c = jnp.split(mod_c, 6, axis=-1)
    p = split_in_proj(modulate(rms_norm(x, norm1_g), sh1, sc1) @ w_in)
    pc = split_in_proj(modulate(rms_norm(xc, norm1_g), sh1c, sc1c) @ w_in)

    y_ret, y_ret_c = retention_mixer((p[0], p[1], p[2], p[3]), (pc[0], pc[1], pc[2], pc[3]),
                                     ret_decay_f, ret_decay_b, cos_r, sin_r, with_ctx_out)
    y_mla, y_mla_c = mla_mixer((p[4], p[5], p[6]), (pc[4], pc[5], pc[6]), mla_q_norm_g, w_uq,
                               mla_kv_norm_g, w_ukv, cos_m, sin_m, with_ctx_out)
    y_pool = pool_mixer(p[7], pool_w, pool_scale)

    x = x + g1 * (jnp.concatenate([y_ret, y_mla, y_pool], axis=-1) @ w_out)
    x = x + g2 * conv_ffn(modulate(rms_norm(x, norm2_g), sh2, sc2), w_up, conv_w, conv_b, w_down)
    if with_ctx_out:
        y_pool_c = pool_mixer(pc[7], pool_w, pool_scale)
        xc = xc + g1c * (jnp.concatenate([y_ret_c, y_mla_c, y_pool_c], axis=-1) @ w_out)
        xc = xc + g2c * conv_ffn(modulate(rms_norm(xc, norm2_g), sh2c, sc2c), w_up, conv_w, conv_b, w_down)
    return x, xc


def setup_inputs(seed: int = 0) -> dict:
    key = jax.random.key(seed)
    ks = jax.random.split(key, 24)
    f32 = jnp.float32
    L = DEPTH

    def nrm(k, shape, scale):
        return jax.random.normal(k, shape, f32) * scale

    def gain(k, shape):
        return 1.0 + 0.02 * jax.random.normal(k, shape, f32)

    heads = jnp.arange(RET_HEADS, dtype=f32)
    decay_logit = jnp.log(jnp.exp2(5.0 + heads) - 1.0)
    return {
        'x': nrm(ks[0], (BATCH, SEQ, D_MODEL), 1.0),
        'c': nrm(ks[1], (BATCH, D_MODEL), 1.0),
        'ctx': nrm(ks[2], (BATCH, CTX_LEN, D_MODEL), 1.0),
        'c_ctx': nrm(ks[3], (D_MODEL,), 1.0),
        'w_mod': nrm(ks[4], (L, D_MODEL, 6 * D_MODEL), 0.5 * D_MODEL ** -0.5),
        'b_mod': nrm(ks[5], (L, 6 * D_MODEL), 0.01),
        'norm1_g': gain(ks[6], (L, D_MODEL)),
        'w_in': nrm(ks[7], (L, D_MODEL, IN_WIDTH), D_MODEL ** -0.5),
        'ret_decay_f': decay_logit + nrm(ks[8], (L, RET_HEADS), 0.1),
        'ret_decay_b': decay_logit + nrm(ks[9], (L, RET_HEADS), 0.1),
        'mla_q_norm_g': gain(ks[10], (L, MLA_Q_RANK)),
        'w_uq': nrm(ks[11], (L, MLA_Q_RANK, MLA_HEADS * (MLA_NOPE + MLA_ROPE)), MLA_Q_RANK ** -0.5),
        'mla_kv_norm_g': gain(ks[12], (L, MLA_KV_RANK)),
        'w_ukv': nrm(ks[13], (L, MLA_KV_RANK, MLA_HEADS * (MLA_NOPE + MLA_DV)), MLA_KV_RANK ** -0.5),
        'pool_w': nrm(ks[14], (L, len(POOL_WINDOWS), POOL_GROUP, POOL_GROUP), POOL_GROUP ** -0.5),
        'pool_scale': gain(ks[15], (L, POOL_WIDTH)),
        'w_out': nrm(ks[16], (L, MIX_WIDTH, D_MODEL), MIX_WIDTH ** -0.5),
        'norm2_g': gain(ks[17], (L, D_MODEL)),
        'w_up': nrm(ks[18], (L, D_MODEL, 2 * D_FF), D_MODEL ** -0.5),
        'conv_w': nrm(ks[19], (L, CONV_WIDTH, 2 * D_FF), CONV_WIDTH ** -0.5),
        'conv_b': nrm(ks[20], (L, 2 * D_FF), 0.01),
        'w_down': nrm(ks[21], (L, D_FF, D_MODEL), D_FF ** -0.5),
        'final_norm_g': gain(ks[22], (D_MODEL,)),
    }


def reference(x, c, ctx, c_ctx, w_mod, b_mod, norm1_g, w_in, ret_decay_f, ret_decay_b, mla_q_norm_g, w_uq,
              mla_kv_norm_g, w_ukv, pool_w, pool_scale, w_out, norm2_g, w_up, conv_w, conv_b, w_down,
              final_norm_g):
    n_tokens = x.shape[1]
    ROWS = n_tokens // GRID_W
    cos_r, sin_r = axial_rope_tables(ROWS, RET_DK)
    cos_m, sin_m = axial_rope_tables(ROWS, MLA_ROPE)
    sc = jax.nn.silu(c)
    sc_ctx = jax.nn.silu(c_ctx)
    xc = ctx
    for l in range(DEPTH):
        mod = (sc @ w_mod[l] + b_mod[l])[:, None, :]
        mod_c = (sc_ctx @ w_mod[l] + b_mod[l])[None, None, :]
        x, xc = trunk_layer(x, xc, mod, mod_c, norm1_g[l], w_in[l], ret_decay_f[l], ret_decay_b[l],
                            mla_q_norm_g[l], w_uq[l], mla_kv_norm_g[l], w_ukv[l], pool_w[l], pool_scale[l],
                            w_out[l], norm2_g[l], w_up[l], conv_w[l], conv_b[l], w_down[l],
                            cos_r, sin_r, cos_m, sin_m, l < DEPTH - 1)
    return rms_norm(x, final_norm_g)
```

```python
import functools

import jax
import jax.numpy as jnp
from jax import lax
from jax.experimental import pallas as pl
from jax.experimental.pallas import tpu as pltpu

BF16 = jnp.bfloat16
F32 = jnp.float32

GRID_W = 64
ROPE_THETA = 10000.0
NORM_EPS = 1e-6
RET_HEADS = 4
RET_DK = 32
RET_DV = 64
MLA_HEADS = 8
MLA_Q_RANK = 384
MLA_KV_RANK = 256
MLA_NOPE = 64
MLA_ROPE = 32
MLA_DV = 64
POOL_WINDOWS = (2, 4, 8, 16)
POOL_GROUP = 64

LANES = 128
SUBLANES = 8
BF16_ROWS = 16
VMEM_LIMIT_BYTES = 56 * 1024 * 1024

RET_W = RET_HEADS * RET_DV
RET_QK = RET_HEADS * RET_DK
MLA_W = MLA_HEADS * MLA_DV
POOL_W = len(POOL_WINDOWS) * POOL_GROUP
HEAD_PAD = 128
HALO = 16
RET_CHUNK = 128

P_RET = 0
P_CQ = 1024
P_KR = 1536
P_CKV = 1792
P_POOL = 2048
P_WIDTH = 2304


def _params(*semantics):
    return pltpu.CompilerParams(dimension_semantics=semantics, vmem_limit_bytes=VMEM_LIMIT_BYTES)


def _resident(block_shape, index_map):
    return pl.BlockSpec(block_shape, index_map, pipeline_mode=pl.Buffered(1))


def _rms(x, width):
    return x * lax.rsqrt(jnp.sum(x * x, axis=-1, keepdims=True) * (1.0 / width) + NORM_EPS)


def _silu(x):
    return x * jax.nn.sigmoid(x)


def _log_sigmoid(d):
    return jnp.minimum(d, 0.0) - jnp.log1p(jnp.exp(-jnp.abs(d)))


def _mod_kernel(c_ref, w_ref, b_ref, o_ref):
    s = _silu(c_ref[...])
    o_ref[...] = jnp.dot(s.astype(BF16), w_ref[...], preferred_element_type=F32) + b_ref[...]


def _modulation(c_rows, w_mod, b_mod):
    L, D, W = w_mod.shape
    tn = 1536
    return pl.pallas_call(
        _mod_kernel,
        grid=(L, W // tn),
        in_specs=[
            pl.BlockSpec((SUBLANES, D), lambda l, j: (0, 0)),
            pl.BlockSpec((None, D, tn), lambda l, j: (l, 0, j)),
            pl.BlockSpec((None, 1, tn), lambda l, j: (l, 0, j)),
        ],
        out_specs=pl.BlockSpec((None, SUBLANES, tn), lambda l, j: (l, 0, j)),
        out_shape=jax.ShapeDtypeStruct((L, SUBLANES, W), F32),
        compiler_params=_params("parallel", "parallel"),
        name="modulation",
    )(c_rows, w_mod, b_mod.reshape(L, 1, W))


def _inproj_kernel(x_ref, g_ref, sh_ref, sc_ref, w_ref, o_ref):
    x = x_ref[...]
    h = _rms(x, x.shape[-1]) * g_ref[...]
    h = h * (1.0 + sc_ref[...]) + sh_ref[...]
    o_ref[...] = jnp.dot(h.astype(BF16), w_ref[...], preferred_element_type=F32)


def _mod_index(section, row_of_batch):
    def index(b, *_):
        return (row_of_batch(b) * 6 + section, 0, 0)

    return index


def _in_projection(x, g, mod, row_of_batch, w):
    B, T, D = x.shape
    tm = min(T, 512)
    vec = lambda sec: pl.BlockSpec((None, 1, D), _mod_index(sec, row_of_batch))
    return pl.pallas_call(
        _inproj_kernel,
        grid=(B, T // tm),
        in_specs=[
            pl.BlockSpec((None, tm, D), lambda b, i: (b, i, 0)),
            pl.BlockSpec((1, D), lambda b, i: (0, 0)),
            vec(0),
            vec(1),
            _resident((D, P_WIDTH), lambda b, i: (0, 0)),
        ],
        out_specs=pl.BlockSpec((None, tm, P_WIDTH), lambda b, i: (b, i, 0)),
        out_shape=jax.ShapeDtypeStruct((B, T, P_WIDTH), F32),
        compiler_params=_params("parallel", "parallel"),
        name="in_projection",
    )(x, g.reshape(1, D), mod, mod, w)


def _ret_rope(a, a_swap, cos_ref, sin_ref, rotate):
    if not rotate:
        return a
    return a * cos_ref[...] + a_swap * sin_ref[...]


def _ret_state_kernel(pf_ref, pb_ref, cf_ref, sf_ref, cb_ref, sb_ref, dfl_ref, dbl_ref, dfc_ref, dbc_ref,
                      s0f_ref, s0b_ref, sf_all_ref, sb_all_ref, sf_fin_ref, sb_fin_ref, stf_ref, stb_ref,
                      *, rotate):
    c = RET_CHUNK
    i = pl.program_id(1)

    @pl.when(i == 0)
    def _():
        stf_ref[...] = s0f_ref[...]
        stb_ref[...] = s0b_ref[...]

    sf_all_ref[...] = stf_ref[...]
    sb_all_ref[...] = stb_ref[...]

    n = lax.broadcasted_iota(jnp.int32, (c, RET_QK), 0).astype(F32)
    rows = lax.broadcasted_iota(jnp.int32, (RET_QK, RET_W), 0)
    cols = lax.broadcasted_iota(jnp.int32, (RET_QK, RET_W), 1)
    head_block = (rows // RET_DK) == (cols // RET_DV)

    def update(p_ref, cos_ref, sin_ref, lg_lane, lg_col, pos_pow, st_ref):
        k = _ret_rope(p_ref[:, 0:RET_QK], p_ref[:, RET_QK:2 * RET_QK], cos_ref, sin_ref, rotate)
        k = k * (RET_DK ** -0.5)
        v = p_ref[:, 2 * RET_QK:2 * RET_QK + RET_W]
        kd = (k * jnp.exp(lg_lane * pos_pow)).astype(BF16)
        kv = lax.dot_general(kd, v.astype(BF16), (((0,), (0,)), ((), ())), preferred_element_type=F32)
        st_ref[...] = st_ref[...] * jnp.exp(lg_col * float(c)) + jnp.where(head_block, kv, 0.0)

    update(pf_ref, cf_ref, sf_ref, _log_sigmoid(dfl_ref[...]), _log_sigmoid(dfc_ref[...]), (c - 1.0) - n, stf_ref)
    update(pb_ref, cb_ref, sb_ref, _log_sigmoid(dbl_ref[...]), _log_sigmoid(dbc_ref[...]), n, stb_ref)

    @pl.when(i == pl.num_programs(1) - 1)
    def _():
        sf_fin_ref[...] = stf_ref[...]
        sb_fin_ref[...] = stb_ref[...]


def _ret_out_kernel(p_ref, cos_ref, sin_ref, dfl_ref, dbl_ref, dfw_ref, dbw_ref, sf_ref, sb_ref, ones_ref,
                    o_ref, *, rotate):
    c = RET_CHUNK
    k = _ret_rope(p_ref[:, 0:RET_QK], p_ref[:, RET_QK:2 * RET_QK], cos_ref, sin_ref, rotate) * (RET_DK ** -0.5)
    v = p_ref[:, 2 * RET_QK:2 * RET_QK + RET_W]
    q = _ret_rope(p_ref[:, 512:512 + RET_QK], p_ref[:, 512 + RET_QK:512 + 2 * RET_QK], cos_ref, sin_ref, rotate)
    gate = p_ref[:, 768:768 + RET_W]

    kr = lax.broadcasted_iota(jnp.int32, (RET_HEADS * c, RET_QK), 0) // c
    kl = lax.broadcasted_iota(jnp.int32, (RET_HEADS * c, RET_QK), 1) // RET_DK
    k_stack = jnp.where(kr == kl, jnp.tile(k, (RET_HEADS, 1)), 0.0).astype(BF16)
    vr = lax.broadcasted_iota(jnp.int32, (RET_HEADS * c, RET_W), 0) // c
    vl = lax.broadcasted_iota(jnp.int32, (RET_HEADS * c, RET_W), 1) // RET_DV
    v_stack = jnp.where(vr == vl, jnp.tile(v, (RET_HEADS, 1)), 0.0).astype(BF16)

    lgf_w = _log_sigmoid(dfw_ref[...])
    lgb_w = _log_sigmoid(dbw_ref[...])
    nq = lax.broadcasted_iota(jnp.int32, (c, RET_HEADS * c), 0)
    mk = lax.broadcasted_iota(jnp.int32, (c, RET_HEADS * c), 1) % c
    diff = (nq - mk).astype(F32)
    dmat = jnp.exp(jnp.where(diff >= 0.0, lgf_w, -lgb_w) * diff)

    scores = lax.dot_general(q.astype(BF16), k_stack, (((1,), (1,)), ((), ())), preferred_element_type=F32)
    a = (scores * dmat).astype(BF16)

    n = lax.broadcasted_iota(jnp.int32, (c, RET_QK), 0).astype(F32)
    qf = (q * jnp.exp(_log_sigmoid(dfl_ref[...]) * (n + 1.0))).astype(BF16)
    qb = (q * jnp.exp(_log_sigmoid(dbl_ref[...]) * (float(c) - n))).astype(BF16)
    o = jnp.dot(a, v_stack, preferred_element_type=F32)
    o = o + jnp.dot(qf, sf_ref[...].astype(BF16), preferred_element_type=F32)
    o = o + jnp.dot(qb, sb_ref[...].astype(BF16), preferred_element_type=F32)

    sq = o * o
    hi = sq.astype(BF16)
    lo = (sq - hi.astype(F32)).astype(BF16)
    ssum = jnp.dot(hi, ones_ref[...], preferred_element_type=F32) + jnp.dot(lo, ones_ref[...], preferred_element_type=F32)
    o = o * lax.rsqrt(ssum * (1.0 / RET_DV) + NORM_EPS)
    o_ref[...] = (o * _silu(gate)).astype(o_ref.dtype)


def _retention(p, cos, sin, decay_f, decay_b, s0f, s0b, rotate):
    B, T, _ = p.shape
    c = RET_CHUNK
    nc = T // c
    rep = lambda d, r: jnp.repeat(d.astype(F32), r)
    dfl, dbl = rep(decay_f, RET_DK).reshape(1, RET_QK), rep(decay_b, RET_DK).reshape(1, RET_QK)
    dfc, dbc = dfl.reshape(RET_QK, 1), dbl.reshape(RET_QK, 1)
    dfw, dbw = rep(decay_f, c).reshape(1, RET_HEADS * c), rep(decay_b, c).reshape(1, RET_HEADS * c)
    small = lambda shape: pl.BlockSpec(shape, lambda b, i: (0,) * len(shape))
    rope_f = pl.BlockSpec((c, RET_QK), lambda b, i: (i, 0))
    rope_b = pl.BlockSpec((c, RET_QK), lambda b, i: (nc - 1 - i, 0))
    state = pl.BlockSpec((None, RET_QK, RET_W), lambda b, i: (b, 0, 0))

    sf_all, sb_all, sf_fin, sb_fin = pl.pallas_call(
        functools.partial(_ret_state_kernel, rotate=rotate),
        grid=(B, nc),
        in_specs=[
            pl.BlockSpec((None, c, 512), lambda b, i: (b, i, 0)),
            pl.BlockSpec((None, c, 512), lambda b, i: (b, nc - 1 - i, 0)),
            rope_f, rope_f, rope_b, rope_b,
            small((1, RET_QK)), small((1, RET_QK)), small((RET_QK, 1)), small((RET_QK, 1)),
            state, state,
        ],
        out_specs=[
            pl.BlockSpec((None, None, RET_QK, RET_W), lambda b, i: (b, i, 0, 0)),
            pl.BlockSpec((None, None, RET_QK, RET_W), lambda b, i: (b, nc - 1 - i, 0, 0)),
            state, state,
        ],
        out_shape=[
            jax.ShapeDtypeStruct((B, nc, RET_QK, RET_W), F32),
            jax.ShapeDtypeStruct((B, nc, RET_QK, RET_W), F32),
            jax.ShapeDtypeStruct((B, RET_QK, RET_W), F32),
            jax.ShapeDtypeStruct((B, RET_QK, RET_W), F32),
        ],
        scratch_shapes=[pltpu.VMEM((RET_QK, RET_W), F32), pltpu.VMEM((RET_QK, RET_W), F32)],
        compiler_params=_params("parallel", "arbitrary"),
        name="retention_states",
    )(p, p, cos, sin, cos, sin, dfl, dbl, dfc, dbc, s0f, s0b)

    gi = lax.broadcasted_iota(jnp.int32, (RET_W, RET_W), 0) // RET_DV
    gj = lax.broadcasted_iota(jnp.int32, (RET_W, RET_W), 1) // RET_DV
    ones_bd = (gi == gj).astype(BF16)
    chunk_state = pl.BlockSpec((None, None, RET_QK, RET_W), lambda b, i: (b, i, 0, 0))
    y = pl.pallas_call(
        functools.partial(_ret_out_kernel, rotate=rotate),
        grid=(B, nc),
        in_specs=[
            pl.BlockSpec((None, c, 1024), lambda b, i: (b, i, 0)),
            rope_f, rope_f,
            small((1, RET_QK)), small((1, RET_QK)), small((1, RET_HEADS * c)), small((1, RET_HEADS * c)),
            chunk_state, chunk_state,
            small((RET_W, RET_W)),
        ],
        out_specs=pl.BlockSpec((None, c, RET_W), lambda b, i: (b, i, 0)),
        out_shape=jax.ShapeDtypeStruct((B, T, RET_W), BF16),
        compiler_params=_params("parallel", "parallel"),
        name="retention_outputs",
    )(p, cos, sin, dfl, dbl, dfw, dbw, sf_all, sb_all, ones_bd)
    return y, sf_fin, sb_fin


def _pool_kernel(prev_ref, x_ref, next_ref, w_ref, scale_ref, o_ref, *, tp, t_total):
    i = pl.program_id(1)
    x = x_ref[...]
    prev = jnp.where(i > 0, prev_ref[...], 0.0)
    nxt = jnp.where(i < pl.num_programs(1) - 1, next_ref[...], 0.0)
    xp = jnp.concatenate([prev, x, nxt], axis=0)
    rows = tp + 2 * HALO
    back = lambda a, s: pltpu.roll(a, s, axis=0)
    ahead = lambda a, s: pltpu.roll(a, rows - s, axis=0)
    a2 = xp + back(xp, 1)
    a4 = a2 + back(a2, 2)
    a8 = a4 + back(a4, 4)
    a16 = a8 + back(a8, 8)
    lane = lax.broadcasted_iota(jnp.int32, (rows, POOL_W), 1)
    group = lane // POOL_GROUP
    win = jnp.where(group == 0, a2, jnp.where(group == 1, ahead(a4, 1), jnp.where(group == 2, ahead(a8, 3), ahead(a16, 7))))
    win = win[HALO:HALO + tp]

    lane = lax.broadcasted_iota(jnp.int32, (tp, POOL_W), 1)
    half = jnp.left_shift(1, lane // POOL_GROUP)
    t = lax.broadcasted_iota(jnp.int32, (tp, POOL_W), 0) + i * tp
    count = jnp.minimum(t + half, t_total) - jnp.maximum(t - half, 0)
    d = win / count.astype(F32) - x
    y = jnp.dot(d.astype(BF16), w_ref[...], preferred_element_type=F32) * scale_ref[...]
    o_ref[...] = y.astype(o_ref.dtype)


def _pool(p, w_bd, scale):
    B, T, _ = p.shape
    tp = min(T, 1024)
    col = P_POOL // POOL_W
    hb = tp // HALO
    return pl.pallas_call(
        functools.partial(_pool_kernel, tp=tp, t_total=T),
        grid=(B, T // tp),
        in_specs=[
            pl.BlockSpec((None, HALO, POOL_W), lambda b, i: (b, jnp.maximum(i * hb - 1, 0), col)),
            pl.BlockSpec((None, tp, POOL_W), lambda b, i: (b, i, col)),
            pl.BlockSpec((None, HALO, POOL_W), lambda b, i: (b, jnp.minimum((i + 1) * hb, T // HALO - 1), col)),
            pl.BlockSpec((POOL_W, POOL_W), lambda b, i: (0, 0)),
            pl.BlockSpec((1, POOL_W), lambda b, i: (0, 0)),
        ],
        out_specs=pl.BlockSpec((None, tp, POOL_W), lambda b, i: (b, i, 0)),
        out_shape=jax.ShapeDtypeStruct((B, T, POOL_W), BF16),
        compiler_params=_params("parallel", "parallel"),
        name="pooling",
    )(p, p, p, w_bd, scale.reshape(1, POOL_W))


def _mla_proj_kernel(cq_ref, kr_ref, ckv_ref, gq_ref, gkv_ref, wq_ref, wkv_ref, cos_ref, sin_ref,
                     q_ref, k_ref, v_ref, *, rotate):
    scale = (MLA_NOPE + MLA_ROPE) ** -0.5
    cq = _rms(cq_ref[...], MLA_Q_RANK) * gq_ref[...]
    q = jnp.dot(cq.astype(BF16), wq_ref[...], preferred_element_type=F32)
    ckv = _rms(ckv_ref[...], MLA_KV_RANK) * gkv_ref[...]
    kv = jnp.dot(ckv.astype(BF16), wkv_ref[...], preferred_element_type=F32)
    kr = kr_ref[:, 0:HEAD_PAD]
    if rotate:
        kr = kr * cos_ref[...] + kr_ref[:, HEAD_PAD:2 * HEAD_PAD] * sin_ref[...]
    ones_col = (lax.broadcasted_iota(jnp.int32, (1, HEAD_PAD), 1) == MLA_DV).astype(F32)
    for h in range(MLA_HEADS):
        base = 2 * HEAD_PAD * h
        qh = q[:, base:base + HEAD_PAD]
        if rotate:
            qh = qh * cos_ref[...] + q[:, base + HEAD_PAD:base + 2 * HEAD_PAD] * sin_ref[...]
        q_ref[h] = (qh * scale).astype(BF16)
        k_ref[h] = (kv[:, base:base + HEAD_PAD] + kr).astype(BF16)
        v_ref[h] = (kv[:, base + HEAD_PAD:base + 2 * HEAD_PAD] + ones_col).astype(BF16)


def _mla_projection(p, gq, gkv, wq, wkv, cos, sin, rotate):
    B, T, _ = p.shape
    tm = min(T, 256)
    heads = pl.BlockSpec((None, MLA_HEADS, tm, HEAD_PAD), lambda b, i: (b, 0, i, 0))
    shape = jax.ShapeDtypeStruct((B, MLA_HEADS, T, HEAD_PAD), BF16)
    const = lambda shape_: _resident(shape_, lambda b, i: (0, 0))
    return pl.pallas_call(
        functools.partial(_mla_proj_kernel, rotate=rotate),
        grid=(B, T // tm),
        in_specs=[
            pl.BlockSpec((None, tm, 512), lambda b, i: (b, i, P_CQ // 512)),
            pl.BlockSpec((None, tm, 256), lambda b, i: (b, i, P_KR // 256)),
            pl.BlockSpec((None, tm, 256), lambda b, i: (b, i, P_CKV // 256)),
            const((1, 512)), const((1, 256)), const((512, 2 * HEAD_PAD * MLA_HEADS)),
            const((256, 2 * HEAD_PAD * MLA_HEADS)),
            pl.BlockSpec((tm, HEAD_PAD), lambda b, i: (i, 0)),
            pl.BlockSpec((tm, HEAD_PAD), lambda b, i: (i, 0)),
        ],
        out_specs=[heads, heads, heads],
        out_shape=[shape, shape, shape],
        compiler_params=_params("parallel", "parallel"),
        name="mla_projection",
    )(p, p, p, gq, gkv, wq, wkv, cos, sin)


def _attn_kernel(q_ref, kc_ref, vc_ref, *rest, tk, n_lat, heads):
    if n_lat:
        kl_ref, vl_ref, o_ref = rest
    else:
        (o_ref,) = rest
    tq = q_ref.shape[1]
    outs = []
    for h in range(heads):
        q = q_ref[h]

        def step(k, v, carry):
            m, acc = carry
            s = lax.dot_general(q, k, (((1,), (1,)), ((), ())), preferred_element_type=F32)
            m_new = jnp.maximum(m, jnp.max(s, axis=-1, keepdims=True))
            p = jnp.exp(s - m_new)
            acc = jnp.exp(m - m_new) * acc + jnp.dot(p.astype(BF16), v, preferred_element_type=F32)
            return m_new, acc

        carry = (jnp.full((tq, 1), -1e30, F32), jnp.zeros((tq, HEAD_PAD), F32))
        carry = step(kc_ref[h], vc_ref[h], carry)
        if n_lat:
            def body(j, carry, h=h, step=step):
                off = pl.multiple_of(j * tk, tk)
                return step(kl_ref[h, pl.ds(off, tk), :], vl_ref[h, pl.ds(off, tk), :], carry)

            carry = lax.fori_loop(0, n_lat, body, carry)
        _, acc = carry
        outs.append(acc[:, 0:MLA_DV] / acc[:, MLA_DV:MLA_DV + 1])
    o_ref[...] = jnp.concatenate(outs, axis=-1).astype(o_ref.dtype)


def _attention(q, k_ctx, v_ctx, k_lat=None, v_lat=None):
    B, H, Tq, _ = q.shape
    Tc = k_ctx.shape[2]
    heads = 2
    tq = min(Tq, 256)
    tk = 512
    n_lat = 0 if k_lat is None else k_lat.shape[2] // tk
    whole = lambda t: pl.BlockSpec((None, heads, t, HEAD_PAD), lambda b, g, i: (b, g, 0, 0))
    in_specs = [pl.BlockSpec((None, heads, tq, HEAD_PAD), lambda b, g, i: (b, g, i, 0)), whole(Tc), whole(Tc)]
    args = [q, k_ctx, v_ctx]
    if n_lat:
        in_specs += [whole(k_lat.shape[2]), whole(k_lat.shape[2])]
        args += [k_lat, v_lat]
    return pl.pallas_call(
        functools.partial(_attn_kernel, tk=tk, n_lat=n_lat, heads=heads),
        grid=(B, H // heads, Tq // tq),
        in_specs=in_specs,
        out_specs=pl.BlockSpec((None, tq, heads * MLA_DV), lambda b, g, i: (b, i, g)),
        out_shape=jax.ShapeDtypeStruct((B, Tq, H * MLA_DV), BF16),
        compiler_params=_params("parallel", "parallel", "parallel"),
        name="attention",
    )(*args)


def _outproj_kernel(x_ref, yr_ref, ym_ref, yp_ref, gate_ref, wr_ref, wm_ref, wp_ref, o_ref):
    y = jnp.dot(yr_ref[...], wr_ref[...], preferred_element_type=F32)
    y = y + jnp.dot(ym_ref[...], wm_ref[...], preferred_element_type=F32)
    y = y + jnp.dot(yp_ref[...], wp_ref[...], preferred_element_type=F32)
    o_ref[...] = x_ref[...] + gate_ref[...] * y


def _out_projection(x, y_ret, y_mla, y_pool, mod, row_of_batch, w_r, w_m, w_p):
    B, T, D = x.shape
    tm = min(T, 512)
    rows = lambda w: pl.BlockSpec((None, tm, w), lambda b, i: (b, i, 0))
    const = lambda shape: _resident(shape, lambda b, i: (0, 0))
    return pl.pallas_call(
        _outproj_kernel,
        grid=(B, T // tm),
        in_specs=[
            rows(D), rows(RET_W), rows(MLA_W), rows(POOL_W),
            pl.BlockSpec((None, 1, D), _mod_index(2, row_of_batch)),
            const((RET_W, D)), const((MLA_W, D)), const((POOL_W, D)),
        ],
        out_specs=rows(D),
        out_shape=jax.ShapeDtypeStruct((B, T, D), F32),
        compiler_params=_params("parallel", "parallel"),
        name="out_projection",
    )(x, y_ret, y_mla, y_pool, mod, w_r, w_m, w_p)


def _ffn_kernel(xp_ref, x_ref, xn_ref, g_ref, sh_ref, sc_ref, gate_ref, wa_ref, wb_ref, cwa_ref, cwb_ref,
                cba_ref, cbb_ref, wd_ref, fg_ref, o_ref, h_ref, acc_ref, *, tm, n_chunks, final):
    i = pl.program_id(1)
    d = x_ref.shape[-1]

    def norm_mod(x):
        h = _rms(x, d) * g_ref[...]
        return h * (1.0 + sc_ref[...]) + sh_ref[...]

    h_ref[0:HALO, :] = jnp.where(i > 0, norm_mod(xp_ref[...]), 0.0).astype(BF16)
    h_ref[HALO:HALO + tm, :] = norm_mod(x_ref[...]).astype(BF16)
    h_ref[HALO + tm:2 * HALO + tm, :] = jnp.where(i < pl.num_programs(1) - 1, norm_mod(xn_ref[...]), 0.0).astype(BF16)
    acc_ref[...] = jnp.zeros_like(acc_ref)
    rows = tm + 2 * HALO

    def conv(u, cw_ref, cb_ref, f):
        cw = cw_ref[f]
        y = pltpu.roll(u, 1, axis=0) * cw[0:1] + u * cw[1:2] + pltpu.roll(u, rows - 1, axis=0) * cw[2:3]
        return y[HALO:HALO + tm] + cb_ref[f]

    def chunk(f, carry):
        h = h_ref[...]
        a = conv(jnp.dot(h, wa_ref[f], preferred_element_type=F32), cwa_ref, cba_ref, f)
        b = conv(jnp.dot(h, wb_ref[f], preferred_element_type=F32), cwb_ref, cbb_ref, f)
        act = (_silu(a) * b).astype(BF16)
        acc_ref[...] += jnp.dot(act, wd_ref[f], preferred_element_type=F32)
        return carry

    lax.fori_loop(0, n_chunks, chunk, 0)
    y = x_ref[...] + gate_ref[...] * acc_ref[...]
    if final:
        y = _rms(y, d) * fg_ref[...]
    o_ref[...] = y


def _ffn(x, g, mod, row_of_batch, wa, wb, cwa, cwb, cba, cbb, wd, final_g, final):
    B, T, D = x.shape
    n_chunks, _, fc = wa.shape
    tm = min(T, 512)
    hb = tm // HALO
    vec = lambda sec: pl.BlockSpec((None, 1, D), _mod_index(sec, row_of_batch))
    const3 = lambda a: _resident(a.shape, lambda b, i: (0, 0, 0))
    return pl.pallas_call(
        functools.partial(_ffn_kernel, tm=tm, n_chunks=n_chunks, final=final),
        grid=(B, T // tm),
        in_specs=[
            pl.BlockSpec((None, HALO, D), lambda b, i: (b, jnp.maximum(i * hb - 1, 0), 0)),
            pl.BlockSpec((None, tm, D), lambda b, i: (b, i, 0)),
            pl.BlockSpec((None, HALO, D), lambda b, i: (b, jnp.minimum((i + 1) * hb, T // HALO - 1), 0)),
            pl.BlockSpec((1, D), lambda b, i: (0, 0)),
            vec(3), vec(4), vec(5),
            const3(wa), const3(wb), const3(cwa), const3(cwb), const3(cba), const3(cbb), const3(wd),
            pl.BlockSpec((1, D), lambda b, i: (0, 0)),
        ],
        out_specs=pl.BlockSpec((None, tm, D), lambda b, i: (b, i, 0)),
        out_shape=jax.ShapeDtypeStruct((B, T, D), F32),
        scratch_shapes=[pltpu.VMEM((tm + 2 * HALO, D), BF16), pltpu.VMEM((tm, D), F32)],
        compiler_params=_params("parallel", "parallel"),
        name="conv_ffn",
    )(x, x, x, g.reshape(1, D), mod, mod, mod, wa, wb, cwa, cwb, cba, cbb, wd, final_g.reshape(1, D))


def _swap_halves(w, groups, width):
    lead = w.shape[:-1]
    return jnp.flip(w.reshape(*lead, groups, 2, width // 2), axis=-2).reshape(*lead, groups * width)


def _layout_in_proj(w_in):
    D = w_in.shape[0]
    offs = [0]
    for s in (RET_QK, RET_QK, RET_W, RET_W, MLA_Q_RANK, MLA_KV_RANK, MLA_ROPE, POOL_W):
        offs.append(offs[-1] + s)
    rq, rk, rv, rg, cq, ckv, kr, pool = [w_in[:, offs[j]:offs[j + 1]] for j in range(8)]
    z = lambda n: jnp.zeros((D, n), w_in.dtype)
    place = lambda a: jnp.concatenate([z(MLA_NOPE), a, z(HEAD_PAD - MLA_NOPE - MLA_ROPE)], axis=1)
    cols = [rk, _swap_halves(rk, RET_HEADS, RET_DK), rv, rq, _swap_halves(rq, RET_HEADS, RET_DK), rg,
            cq, z(512 - MLA_Q_RANK), place(kr), place(_swap_halves(kr, 1, MLA_ROPE)), ckv, pool]
    return jnp.concatenate(cols, axis=1).astype(BF16)


def _layout_uq(w_uq):
    r = w_uq.shape[0]
    w = w_uq.reshape(r, MLA_HEADS, MLA_NOPE + MLA_ROPE)
    nope, rope = w[..., :MLA_NOPE], w[..., MLA_NOPE:]
    z = lambda n: jnp.zeros((r, MLA_HEADS, n), w_uq.dtype)
    tail = HEAD_PAD - MLA_NOPE - MLA_ROPE
    per_head = jnp.concatenate([nope, rope, z(tail), z(MLA_NOPE), _swap_halves(rope, 1, MLA_ROPE), z(tail)], axis=-1)
    w = per_head.reshape(r, MLA_HEADS * 2 * HEAD_PAD)
    return jnp.concatenate([w, jnp.zeros((512 - r, w.shape[1]), w.dtype)], axis=0).astype(BF16)


def _layout_ukv(w_ukv):
    r = w_ukv.shape[0]
    w = w_ukv.reshape(r, MLA_HEADS, MLA_NOPE + MLA_DV)
    z = lambda n: jnp.zeros((r, MLA_HEADS, n), w_ukv.dtype)
    per_head = jnp.concatenate([w[..., :MLA_NOPE], z(HEAD_PAD - MLA_NOPE), w[..., MLA_NOPE:], z(HEAD_PAD - MLA_DV)], axis=-1)
    return per_head.reshape(r, MLA_HEADS * 2 * HEAD_PAD).astype(BF16)


def _chunk_cols(w, fc):
    r, f = w.shape
    return w.reshape(r, f // fc, fc).transpose(1, 0, 2)


def _rope_tables(t):
    rows = t // GRID_W
    row = jnp.repeat(jnp.arange(rows, dtype=F32), GRID_W)
    col = jnp.tile(jnp.arange(GRID_W, dtype=F32), rows)
    n_freq = MLA_ROPE // 4
    inv = ROPE_THETA ** (-jnp.arange(n_freq, dtype=F32) / n_freq)
    ang = jnp.concatenate([row[:, None] * inv, col[:, None] * inv], axis=-1)
    cos, sin = jnp.cos(ang), jnp.sin(ang)
    cc = jnp.concatenate([cos, cos], axis=-1)
    ss = jnp.concatenate([-sin, sin], axis=-1)
    ret_cos, ret_sin = jnp.tile(cc, (1, RET_HEADS)), jnp.tile(ss, (1, RET_HEADS))
    tail = HEAD_PAD - MLA_NOPE - MLA_ROPE
    one, zero = jnp.ones((t, MLA_NOPE), F32), jnp.zeros((t, MLA_NOPE), F32)
    mla_cos = jnp.concatenate([one, cc, jnp.ones((t, tail), F32)], axis=-1)
    mla_sin = jnp.concatenate([zero, ss, jnp.zeros((t, tail), F32)], axis=-1)
    return ret_cos, ret_sin, mla_cos, mla_sin


def kernel(x, c, ctx, c_ctx, w_mod, b_mod, norm1_g, w_in, ret_decay_f, ret_decay_b, mla_q_norm_g, w_uq,
           mla_kv_norm_g, w_ukv, pool_w, pool_scale, w_out, norm2_g, w_up, conv_w, conv_b, w_down,
           final_norm_g):
    B, T, D = x.shape
    L = w_mod.shape[0]
    d_ff = w_down.shape[1]
    fc = 256
    assert T % 512 == 0 and ctx.shape[1] % RET_CHUNK == 0 and d_ff % fc == 0 and B + 1 <= SUBLANES

    ret_cos, ret_sin, mla_cos, mla_sin = _rope_tables(T)
    c_rows = jnp.concatenate([c, c_ctx[None, :], jnp.zeros((SUBLANES - B - 1, D), c.dtype)], axis=0)
    mod_all = _modulation(c_rows, w_mod.astype(BF16), b_mod)
    lat_row = lambda b: b
    ctx_row = lambda b: B

    xc = ctx
    zero_state = jnp.zeros((B, RET_QK, RET_W), F32)
    for l in range(L):
        with_ctx_out = l < L - 1
        mod = mod_all[l].reshape(SUBLANES * 6, 1, D)
        w_p = _layout_in_proj(w_in[l])
        wq = _layout_uq(w_uq[l])
        wkv = _layout_ukv(w_ukv[l])
        gq = jnp.concatenate([mla_q_norm_g[l], jnp.zeros((512 - MLA_Q_RANK,), F32)]).reshape(1, 512)
        gkv = mla_kv_norm_g[l].reshape(1, MLA_KV_RANK)
        pool_bd = jax.scipy.linalg.block_diag(*[pool_w[l, g] for g in range(len(POOL_WINDOWS))]).astype(BF16)
        wo = w_out[l].astype(BF16)
        wo_r, wo_m, wo_p = wo[:RET_W], wo[RET_W:RET_W + MLA_W], wo[RET_W + MLA_W:]
        wa, wb = _chunk_cols(w_up[l, :, :d_ff].astype(BF16), fc), _chunk_cols(w_up[l, :, d_ff:].astype(BF16), fc)
        cwa, cwb = _chunk_cols(conv_w[l, :, :d_ff], fc), _chunk_cols(conv_w[l, :, d_ff:], fc)
        cba, cbb = _chunk_cols(conv_b[l, None, :d_ff], fc), _chunk_cols(conv_b[l, None, d_ff:], fc)
        wd = w_down[l].astype(BF16).reshape(d_ff // fc, fc, D)

        p_lat = _in_projection(x, norm1_g[l], mod, lat_row, w_p)
        p_ctx = _in_projection(xc, norm1_g[l], mod, ctx_row, w_p)

        yr_ctx, s_f, s_b = _retention(p_ctx, ret_cos, ret_sin, ret_decay_f[l], ret_decay_b[l], zero_state, zero_state, False)
        yr_lat, _, _ = _retention(p_lat, ret_cos, ret_sin, ret_decay_f[l], ret_decay_b[l], s_f, s_b, True)

        q_ctx, k_ctx, v_ctx = _mla_projection(p_ctx, gq, gkv, wq, wkv, mla_cos, mla_sin, False)
        q_lat, k_lat, v_lat = _mla_projection(p_lat, gq, gkv, wq, wkv, mla_cos, mla_sin, True)
        ym_lat = _attention(q_lat, k_ctx, v_ctx, k_lat, v_lat)

        yp_lat = _pool(p_lat, pool_bd, pool_scale[l])
        x = _out_projection(x, yr_lat, ym_lat, yp_lat, mod, lat_row, wo_r, wo_m, wo_p)
        x = _ffn(x, norm2_g[l], mod, lat_row, wa, wb, cwa, cwb, cba, cbb, wd, final_norm_g, not with_ctx_out)
        if with_ctx_out:
            ym_ctx = _attention(q_ctx, k_ctx, v_ctx)
            yp_ctx = _pool(p_ctx, pool_bd, pool_scale[l])
            xc = _out_projection(xc, yr_ctx, ym_ctx, yp_ctx, mod, ctx_row, wo_r, wo_m, wo_p)
            xc = _ffn(xc, norm2_g[l], mod, ctx_row, wa, wb, cwa, cwb, cba, cbb, wd, final_norm_g, False)
    return x
```

```python
import functools

import jax
import jax.numpy as jnp
from jax import lax
from jax.experimental import pallas as pl
from jax.experimental.pallas import tpu as pltpu

BF16 = jnp.bfloat16
F32 = jnp.float32

GRID_W = 64
ROPE_THETA = 10000.0
NORM_EPS = 1e-6
RET_HEADS = 4
RET_DK = 32
RET_DV = 64
MLA_HEADS = 8
MLA_Q_RANK = 384
MLA_KV_RANK = 256
MLA_NOPE = 64
MLA_ROPE = 32
MLA_DV = 64
POOL_WINDOWS = (2, 4, 8, 16)
POOL_GROUP = 64

LANES = 128
SUBLANES = 8
BF16_ROWS = 16
VMEM_LIMIT_BYTES = 56 * 1024 * 1024

RET_W = RET_HEADS * RET_DV
RET_QK = RET_HEADS * RET_DK
MLA_W = MLA_HEADS * MLA_DV
POOL_W = len(POOL_WINDOWS) * POOL_GROUP
HEAD_PAD = 128
HALO = 16
LOG2_E = 1.4426950408889634
RET_CHUNK = 128

P_RET = 0
P_CQ = 1024
P_KR = 1536
P_CKV = 1792
P_POOL = 2048
P_WIDTH = 2304


def _params(*semantics):
    return pltpu.CompilerParams(dimension_semantics=semantics, vmem_limit_bytes=VMEM_LIMIT_BYTES)


def _resident(block_shape, index_map):
    return pl.BlockSpec(block_shape, index_map, pipeline_mode=pl.Buffered(1))


def _rms(x, width):
    return x * lax.rsqrt(jnp.sum(x * x, axis=-1, keepdims=True) * (1.0 / width) + NORM_EPS)


def _silu(x):
    return x * jax.nn.sigmoid(x)


def _log_sigmoid(d):
    return jnp.minimum(d, 0.0) - jnp.log1p(jnp.exp(-jnp.abs(d)))


def _mod_kernel(c_ref, w_ref, b_ref, o_ref):
    s = _silu(c_ref[...])
    o_ref[...] = jnp.dot(s.astype(BF16), w_ref[...], preferred_element_type=F32) + b_ref[...]


def _modulation(c_rows, w_mod, b_mod):
    L, D, W = w_mod.shape
    tn = 1536
    return pl.pallas_call(
        _mod_kernel,
        grid=(L, W // tn),
        in_specs=[
            pl.BlockSpec((SUBLANES, D), lambda l, j: (0, 0)),
            pl.BlockSpec((None, D, tn), lambda l, j: (l, 0, j)),
            pl.BlockSpec((None, 1, tn), lambda l, j: (l, 0, j)),
        ],
        out_specs=pl.BlockSpec((None, SUBLANES, tn), lambda l, j: (l, 0, j)),
        out_shape=jax.ShapeDtypeStruct((L, SUBLANES, W), F32),
        compiler_params=_params("parallel", "parallel"),
        name="modulation",
    )(c_rows, w_mod, b_mod.reshape(L, 1, W))


def _inproj_kernel(x_ref, g_ref, sh_ref, sc_ref, w_ref, o_ref):
    x = x_ref[...]
    h = _rms(x, x.shape[-1]) * g_ref[...]
    h = h * (1.0 + sc_ref[...]) + sh_ref[...]
    o_ref[...] = jnp.dot(h.astype(BF16), w_ref[...], preferred_element_type=F32)


def _mod_index(section, row_of_batch):
    def index(b, *_):
        return (row_of_batch(b) * 6 + section, 0, 0)

    return index


def _in_projection(x, g, mod, row_of_batch, w):
    B, T, D = x.shape
    tm = min(T, 512)
    vec = lambda sec: pl.BlockSpec((None, 1, D), _mod_index(sec, row_of_batch))
    return pl.pallas_call(
        _inproj_kernel,
        grid=(B, T // tm),
        in_specs=[
            pl.BlockSpec((None, tm, D), lambda b, i: (b, i, 0)),
            pl.BlockSpec((1, D), lambda b, i: (0, 0)),
            vec(0),
            vec(1),
            _resident((D, P_WIDTH), lambda b, i: (0, 0)),
        ],
        out_specs=pl.BlockSpec((None, tm, P_WIDTH), lambda b, i: (b, i, 0)),
        out_shape=jax.ShapeDtypeStruct((B, T, P_WIDTH), F32),
        compiler_params=_params("parallel", "parallel"),
        name="in_projection",
    )(x, g.reshape(1, D), mod, mod, w)


def _ret_rope(a, a_swap, cos_ref, sin_ref, rotate):
    if not rotate:
        return a
    return a * cos_ref[...] + a_swap * sin_ref[...]


def _ret_state_kernel(pf_ref, pb_ref, cf_ref, sf_ref, cb_ref, sb_ref, dfl_ref, dbl_ref, dfc_ref, dbc_ref,
                      s0f_ref, s0b_ref, sf_all_ref, sb_all_ref, sf_fin_ref, sb_fin_ref, stf_ref, stb_ref,
                      *, rotate):
    c = RET_CHUNK
    i = pl.program_id(1)

    @pl.when(i == 0)
    def _():
        stf_ref[...] = s0f_ref[...]
        stb_ref[...] = s0b_ref[...]

    sf_all_ref[...] = stf_ref[...]
    sb_all_ref[...] = stb_ref[...]

    n = lax.broadcasted_iota(jnp.int32, (c, RET_QK), 0).astype(F32)
    rows = lax.broadcasted_iota(jnp.int32, (RET_QK, RET_W), 0)
    cols = lax.broadcasted_iota(jnp.int32, (RET_QK, RET_W), 1)
    head_block = (rows // RET_DK) == (cols // RET_DV)

    def update(p_ref, cos_ref, sin_ref, lg_lane, lg_col, pos_pow, st_ref):
        k = _ret_rope(p_ref[:, 0:RET_QK], p_ref[:, RET_QK:2 * RET_QK], cos_ref, sin_ref, rotate)
        k = k * (RET_DK ** -0.5)
        v = p_ref[:, 2 * RET_QK:2 * RET_QK + RET_W]
        kd = (k * jnp.exp(lg_lane * pos_pow)).astype(BF16)
        kv = lax.dot_general(kd, v.astype(BF16), (((0,), (0,)), ((), ())), preferred_element_type=F32)
        st_ref[...] = st_ref[...] * jnp.exp(lg_col * float(c)) + jnp.where(head_block, kv, 0.0)

    update(pf_ref, cf_ref, sf_ref, _log_sigmoid(dfl_ref[...]), _log_sigmoid(dfc_ref[...]), (c - 1.0) - n, stf_ref)
    update(pb_ref, cb_ref, sb_ref, _log_sigmoid(dbl_ref[...]), _log_sigmoid(dbc_ref[...]), n, stb_ref)

    @pl.when(i == pl.num_programs(1) - 1)
    def _():
        sf_fin_ref[...] = stf_ref[...]
        sb_fin_ref[...] = stb_ref[...]


def _ret_out_kernel(p_ref, cos_ref, sin_ref, dfl_ref, dbl_ref, dfw_ref, dbw_ref, sf_ref, sb_ref, ones_ref,
                    o_ref, *, rotate):
    c = RET_CHUNK
    k = _ret_rope(p_ref[:, 0:RET_QK], p_ref[:, RET_QK:2 * RET_QK], cos_ref, sin_ref, rotate) * (RET_DK ** -0.5)
    v = p_ref[:, 2 * RET_QK:2 * RET_QK + RET_W]
    q = _ret_rope(p_ref[:, 512:512 + RET_QK], p_ref[:, 512 + RET_QK:512 + 2 * RET_QK], cos_ref, sin_ref, rotate)
    gate = p_ref[:, 768:768 + RET_W]

    kr = lax.broadcasted_iota(jnp.int32, (RET_HEADS * c, RET_QK), 0) // c
    kl = lax.broadcasted_iota(jnp.int32, (RET_HEADS * c, RET_QK), 1) // RET_DK
    k_stack = jnp.where(kr == kl, jnp.tile(k, (RET_HEADS, 1)), 0.0).astype(BF16)
    vr = lax.broadcasted_iota(jnp.int32, (RET_HEADS * c, RET_W), 0) // c
    vl = lax.broadcasted_iota(jnp.int32, (RET_HEADS * c, RET_W), 1) // RET_DV
    v_stack = jnp.where(vr == vl, jnp.tile(v, (RET_HEADS, 1)), 0.0).astype(BF16)

    lgf_w = _log_sigmoid(dfw_ref[...])
    lgb_w = _log_sigmoid(dbw_ref[...])
    nq = lax.broadcasted_iota(jnp.int32, (c, RET_HEADS * c), 0)
    mk = lax.broadcasted_iota(jnp.int32, (c, RET_HEADS * c), 1) % c
    diff = (nq - mk).astype(F32)
    dmat = jnp.exp(jnp.where(diff >= 0.0, lgf_w, -lgb_w) * diff)

    scores = lax.dot_general(q.astype(BF16), k_stack, (((1,), (1,)), ((), ())), preferred_element_type=F32)
    a = (scores * dmat).astype(BF16)

    n = lax.broadcasted_iota(jnp.int32, (c, RET_QK), 0).astype(F32)
    qf = (q * jnp.exp(_log_sigmoid(dfl_ref[...]) * (n + 1.0))).astype(BF16)
    qb = (q * jnp.exp(_log_sigmoid(dbl_ref[...]) * (float(c) - n))).astype(BF16)
    o = jnp.dot(a, v_stack, preferred_element_type=F32)
    o = o + jnp.dot(qf, sf_ref[...].astype(BF16), preferred_element_type=F32)
    o = o + jnp.dot(qb, sb_ref[...].astype(BF16), preferred_element_type=F32)

    sq = o * o
    hi = sq.astype(BF16)
    lo = (sq - hi.astype(F32)).astype(BF16)
    ssum = jnp.dot(hi, ones_ref[...], preferred_element_type=F32) + jnp.dot(lo, ones_ref[...], preferred_element_type=F32)
    o = o * lax.rsqrt(ssum * (1.0 / RET_DV) + NORM_EPS)
    o_ref[...] = (o * _silu(gate)).astype(o_ref.dtype)


def _retention(p, cos, sin, decay_f, decay_b, s0f, s0b, rotate):
    B, T, _ = p.shape
    c = RET_CHUNK
    nc = T // c
    rep = lambda d, r: jnp.repeat(d.astype(F32), r)
    dfl, dbl = rep(decay_f, RET_DK).reshape(1, RET_QK), rep(decay_b, RET_DK).reshape(1, RET_QK)
    dfc, dbc = dfl.reshape(RET_QK, 1), dbl.reshape(RET_QK, 1)
    dfw, dbw = rep(decay_f, c).reshape(1, RET_HEADS * c), rep(decay_b, c).reshape(1, RET_HEADS * c)
    small = lambda shape: pl.BlockSpec(shape, lambda b, i: (0,) * len(shape))
    rope_f = pl.BlockSpec((c, RET_QK), lambda b, i: (i, 0))
    rope_b = pl.BlockSpec((c, RET_QK), lambda b, i: (nc - 1 - i, 0))
    state = pl.BlockSpec((None, RET_QK, RET_W), lambda b, i: (b, 0, 0))

    sf_all, sb_all, sf_fin, sb_fin = pl.pallas_call(
        functools.partial(_ret_state_kernel, rotate=rotate),
        grid=(B, nc),
        in_specs=[
            pl.BlockSpec((None, c, 512), lambda b, i: (b, i, 0)),
            pl.BlockSpec((None, c, 512), lambda b, i: (b, nc - 1 - i, 0)),
            rope_f, rope_f, rope_b, rope_b,
            small((1, RET_QK)), small((1, RET_QK)), small((RET_QK, 1)), small((RET_QK, 1)),
            state, state,
        ],
        out_specs=[
            pl.BlockSpec((None, None, RET_QK, RET_W), lambda b, i: (b, i, 0, 0)),
            pl.BlockSpec((None, None, RET_QK, RET_W), lambda b, i: (b, nc - 1 - i, 0, 0)),
            state, state,
        ],
        out_shape=[
            jax.ShapeDtypeStruct((B, nc, RET_QK, RET_W), F32),
            jax.ShapeDtypeStruct((B, nc, RET_QK, RET_W), F32),
            jax.ShapeDtypeStruct((B, RET_QK, RET_W), F32),
            jax.ShapeDtypeStruct((B, RET_QK, RET_W), F32),
        ],
        scratch_shapes=[pltpu.VMEM((RET_QK, RET_W), F32), pltpu.VMEM((RET_QK, RET_W), F32)],
        compiler_params=_params("parallel", "arbitrary"),
        name="retention_states",
    )(p, p, cos, sin, cos, sin, dfl, dbl, dfc, dbc, s0f, s0b)

    gi = lax.broadcasted_iota(jnp.int32, (RET_W, RET_W), 0) // RET_DV
    gj = lax.broadcasted_iota(jnp.int32, (RET_W, RET_W), 1) // RET_DV
    ones_bd = (gi == gj).astype(BF16)
    chunk_state = pl.BlockSpec((None, None, RET_QK, RET_W), lambda b, i: (b, i, 0, 0))
    y = pl.pallas_call(
        functools.partial(_ret_out_kernel, rotate=rotate),
        grid=(B, nc),
        in_specs=[
            pl.BlockSpec((None, c, 1024), lambda b, i: (b, i, 0)),
            rope_f, rope_f,
            small((1, RET_QK)), small((1, RET_QK)), small((1, RET_HEADS * c)), small((1, RET_HEADS * c)),
            chunk_state, chunk_state,
            small((RET_W, RET_W)),
        ],
        out_specs=pl.BlockSpec((None, c, RET_W), lambda b, i: (b, i, 0)),
        out_shape=jax.ShapeDtypeStruct((B, T, RET_W), BF16),
        compiler_params=_params("parallel", "parallel"),
        name="retention_outputs",
    )(p, cos, sin, dfl, dbl, dfw, dbw, sf_all, sb_all, ones_bd)
    return y, sf_fin, sb_fin


def _pool_kernel(prev_ref, x_ref, next_ref, w_ref, scale_ref, o_ref, *, tp, t_total):
    i = pl.program_id(1)
    x = x_ref[...]
    prev = jnp.where(i > 0, prev_ref[...], 0.0)
    nxt = jnp.where(i < pl.num_programs(1) - 1, next_ref[...], 0.0)
    xp = jnp.concatenate([prev, x, nxt], axis=0)
    rows = tp + 2 * HALO
    back = lambda a, s: pltpu.roll(a, s, axis=0)
    ahead = lambda a, s: pltpu.roll(a, rows - s, axis=0)
    a2 = xp + back(xp, 1)
    a4 = a2 + back(a2, 2)
    a8 = a4 + back(a4, 4)
    a16 = a8 + back(a8, 8)
    lane = lax.broadcasted_iota(jnp.int32, (rows, POOL_W), 1)
    group = lane // POOL_GROUP
    win = jnp.where(group == 0, a2, jnp.where(group == 1, ahead(a4, 1), jnp.where(group == 2, ahead(a8, 3), ahead(a16, 7))))
    win = win[HALO:HALO + tp]

    lane = lax.broadcasted_iota(jnp.int32, (tp, POOL_W), 1)
    half = jnp.left_shift(1, lane // POOL_GROUP)
    t = lax.broadcasted_iota(jnp.int32, (tp, POOL_W), 0) + i * tp
    count = jnp.minimum(t + half, t_total) - jnp.maximum(t - half, 0)
    d = win / count.astype(F32) - x
    y = jnp.dot(d.astype(BF16), w_ref[...], preferred_element_type=F32) * scale_ref[...]
    o_ref[...] = y.astype(o_ref.dtype)


def _pool(p, w_bd, scale):
    B, T, _ = p.shape
    tp = min(T, 1024)
    col = P_POOL // POOL_W
    hb = tp // HALO
    return pl.pallas_call(
        functools.partial(_pool_kernel, tp=tp, t_total=T),
        grid=(B, T // tp),
        in_specs=[
            pl.BlockSpec((None, HALO, POOL_W), lambda b, i: (b, jnp.maximum(i * hb - 1, 0), col)),
            pl.BlockSpec((None, tp, POOL_W), lambda b, i: (b, i, col)),
            pl.BlockSpec((None, HALO, POOL_W), lambda b, i: (b, jnp.minimum((i + 1) * hb, T // HALO - 1), col)),
            pl.BlockSpec((POOL_W, POOL_W), lambda b, i: (0, 0)),
            pl.BlockSpec((1, POOL_W), lambda b, i: (0, 0)),
        ],
        out_specs=pl.BlockSpec((None, tp, POOL_W), lambda b, i: (b, i, 0)),
        out_shape=jax.ShapeDtypeStruct((B, T, POOL_W), BF16),
        compiler_params=_params("parallel", "parallel"),
        name="pooling",
    )(p, p, p, w_bd, scale.reshape(1, POOL_W))


def _mla_proj_kernel(cq_ref, kr_ref, ckv_ref, gq_ref, gkv_ref, wq_ref, wkv_ref, cos_ref, sin_ref,
                     q_ref, k_ref, v_ref, *, rotate):
    scale = (MLA_NOPE + MLA_ROPE) ** -0.5 * LOG2_E
    cq = _rms(cq_ref[...], MLA_Q_RANK) * gq_ref[...]
    q = jnp.dot(cq.astype(BF16), wq_ref[...], preferred_element_type=F32)
    ckv = _rms(ckv_ref[...], MLA_KV_RANK) * gkv_ref[...]
    kv = jnp.dot(ckv.astype(BF16), wkv_ref[...], preferred_element_type=F32)
    kr = kr_ref[:, 0:HEAD_PAD]
    if rotate:
        kr = kr * cos_ref[...] + kr_ref[:, HEAD_PAD:2 * HEAD_PAD] * sin_ref[...]
    ones_col = (lax.broadcasted_iota(jnp.int32, (1, HEAD_PAD), 1) == MLA_DV).astype(F32)
    for h in range(MLA_HEADS):
        base = 2 * HEAD_PAD * h
        qh = q[:, base:base + HEAD_PAD]
        if rotate:
            qh = qh * cos_ref[...] + q[:, base + HEAD_PAD:base + 2 * HEAD_PAD] * sin_ref[...]
        q_ref[h] = (qh * scale).astype(BF16)
        k_ref[h] = (kv[:, base:base + HEAD_PAD] + kr).astype(BF16)
        v_ref[h] = (kv[:, base + HEAD_PAD:base + 2 * HEAD_PAD] + ones_col).astype(BF16)


def _mla_projection(p, gq, gkv, wq, wkv, cos, sin, rotate):
    B, T, _ = p.shape
    tm = min(T, 256)
    heads = pl.BlockSpec((None, MLA_HEADS, tm, HEAD_PAD), lambda b, i: (b, 0, i, 0))
    shape = jax.ShapeDtypeStruct((B, MLA_HEADS, T, HEAD_PAD), BF16)
    const = lambda shape_: _resident(shape_, lambda b, i: (0, 0))
    return pl.pallas_call(
        functools.partial(_mla_proj_kernel, rotate=rotate),
        grid=(B, T // tm),
        in_specs=[
            pl.BlockSpec((None, tm, 512), lambda b, i: (b, i, P_CQ // 512)),
            pl.BlockSpec((None, tm, 256), lambda b, i: (b, i, P_KR // 256)),
            pl.BlockSpec((None, tm, 256), lambda b, i: (b, i, P_CKV // 256)),
            const((1, 512)), const((1, 256)), const((512, 2 * HEAD_PAD * MLA_HEADS)),
            const((256, 2 * HEAD_PAD * MLA_HEADS)),
            pl.BlockSpec((tm, HEAD_PAD), lambda b, i: (i, 0)),
            pl.BlockSpec((tm, HEAD_PAD), lambda b, i: (i, 0)),
        ],
        out_specs=[heads, heads, heads],
        out_shape=[shape, shape, shape],
        compiler_params=_params("parallel", "parallel"),
        name="mla_projection",
    )(p, p, p, gq, gkv, wq, wkv, cos, sin)


def _attn_kernel(q_ref, kc_ref, vc_ref, *rest, tk, n_lat, heads):
    if n_lat:
        kl_ref, vl_ref, o_ref, m_ref, acc_ref, s_ref = rest
    else:
        o_ref, m_ref, acc_ref = rest
    m_ref[...] = jnp.full_like(m_ref, -1e30)
    acc_ref[...] = jnp.zeros_like(acc_ref)

    def scores(h, k):
        return lax.dot_general(q_ref[h], k, (((1,), (1,)), ((), ())), preferred_element_type=F32)

    def update(h, s, v):
        m_old = m_ref[h]
        m_new = jnp.maximum(m_old[:, 0:1], jnp.max(s, axis=-1, keepdims=True))
        p = jnp.exp2(s - m_new)
        acc_ref[h] = jnp.exp2(m_old - m_new) * acc_ref[h] + jnp.dot(p.astype(BF16), v, preferred_element_type=F32)
        m_ref[h] = jnp.broadcast_to(m_new, m_old.shape)

    def lat(ref, h, j):
        return ref[h, pl.ds(pl.multiple_of(j * tk, tk), tk), :]

    def half(j, cur, nxt):
        for h in range(heads):
            if nxt is not None:
                s_ref[nxt, h] = scores(h, lat(kl_ref, h, j + 1))
            update(h, s_ref[cur, h], lat(vl_ref, h, j))

    ctx_scores = [scores(h, kc_ref[h]) for h in range(heads)]
    if n_lat:
        for h in range(heads):
            s_ref[0, h] = scores(h, lat(kl_ref, h, 0))
    for h in range(heads):
        update(h, ctx_scores[h], vc_ref[h])
    if n_lat:
        def body(jj, carry):
            half(2 * jj, 0, 1)
            half(2 * jj + 1, 1, 0)
            return carry

        lax.fori_loop(0, n_lat // 2 - 1, body, 0)
        half(n_lat - 2, 0, 1)
        half(n_lat - 1, 1, None)
    outs = [acc_ref[h][:, 0:MLA_DV] / acc_ref[h][:, MLA_DV:MLA_DV + 1] for h in range(heads)]
    o_ref[...] = jnp.concatenate(outs, axis=-1).astype(o_ref.dtype)


def _attention(q, k_ctx, v_ctx, k_lat=None, v_lat=None):
    B, H, Tq, _ = q.shape
    Tc = k_ctx.shape[2]
    heads = 2
    tq = min(Tq, 256)
    tk = 512
    n_lat = 0 if k_lat is None else k_lat.shape[2] // tk
    whole = lambda t: pl.BlockSpec((None, heads, t, HEAD_PAD), lambda b, g, i: (b, g, 0, 0))
    in_specs = [pl.BlockSpec((None, heads, tq, HEAD_PAD), lambda b, g, i: (b, g, i, 0)), whole(Tc), whole(Tc)]
    args = [q, k_ctx, v_ctx]
    scratch = [pltpu.VMEM((heads, tq, HEAD_PAD), F32), pltpu.VMEM((heads, tq, HEAD_PAD), F32)]
    if n_lat:
        assert n_lat % 2 == 0
        in_specs += [whole(k_lat.shape[2]), whole(k_lat.shape[2])]
        args += [k_lat, v_lat]
        scratch.append(pltpu.VMEM((2, heads, tq, tk), F32))
    return pl.pallas_call(
        functools.partial(_attn_kernel, tk=tk, n_lat=n_lat, heads=heads),
        grid=(B, H // heads, Tq // tq),
        in_specs=in_specs,
        out_specs=pl.BlockSpec((None, tq, heads * MLA_DV), lambda b, g, i: (b, i, g)),
        out_shape=jax.ShapeDtypeStruct((B, Tq, H * MLA_DV), BF16),
        scratch_shapes=scratch,
        compiler_params=_params("parallel", "parallel", "parallel"),
        name="attention",
    )(*args)


def _outproj_kernel(x_ref, yr_ref, ym_ref, yp_ref, gate_ref, wr_ref, wm_ref, wp_ref, o_ref):
    y = jnp.dot(yr_ref[...], wr_ref[...], preferred_element_type=F32)
    y = y + jnp.dot(ym_ref[...], wm_ref[...], preferred_element_type=F32)
    y = y + jnp.dot(yp_ref[...], wp_ref[...], preferred_element_type=F32)
    o_ref[...] = x_ref[...] + gate_ref[...] * y


def _out_projection(x, y_ret, y_mla, y_pool, mod, row_of_batch, w_r, w_m, w_p):
    B, T, D = x.shape
    tm = min(T, 512)
    rows = lambda w: pl.BlockSpec((None, tm, w), lambda b, i: (b, i, 0))
    const = lambda shape: _resident(shape, lambda b, i: (0, 0))
    return pl.pallas_call(
        _outproj_kernel,
        grid=(B, T // tm),
        in_specs=[
            rows(D), rows(RET_W), rows(MLA_W), rows(POOL_W),
            pl.BlockSpec((None, 1, D), _mod_index(2, row_of_batch)),
            const((RET_W, D)), const((MLA_W, D)), const((POOL_W, D)),
        ],
        out_specs=rows(D),
        out_shape=jax.ShapeDtypeStruct((B, T, D), F32),
        compiler_params=_params("parallel", "parallel"),
        name="out_projection",
    )(x, y_ret, y_mla, y_pool, mod, w_r, w_m, w_p)


def _ffn_kernel(xp_ref, x_ref, xn_ref, g_ref, sh_ref, sc_ref, gate_ref, wa_ref, wb_ref, cwa_ref, cwb_ref,
                cba_ref, cbb_ref, wd_ref, fg_ref, o_ref, h_ref, acc_ref, *, tm, n_chunks, final):
    i = pl.program_id(1)
    d = x_ref.shape[-1]

    def norm_mod(x):
        h = _rms(x, d) * g_ref[...]
        return h * (1.0 + sc_ref[...]) + sh_ref[...]

    h_ref[0:HALO, :] = jnp.where(i > 0, norm_mod(xp_ref[...]), 0.0).astype(BF16)
    h_ref[HALO:HALO + tm, :] = norm_mod(x_ref[...]).astype(BF16)
    h_ref[HALO + tm:2 * HALO + tm, :] = jnp.where(i < pl.num_programs(1) - 1, norm_mod(xn_ref[...]), 0.0).astype(BF16)
    acc_ref[...] = jnp.zeros_like(acc_ref)
    rows = tm + 2 * HALO

    def conv(u, cw_ref, cb_ref, f):
        cw = cw_ref[f]
        y = pltpu.roll(u, 1, axis=0) * cw[0:1] + u * cw[1:2] + pltpu.roll(u, rows - 1, axis=0) * cw[2:3]
        return y[HALO:HALO + tm] + cb_ref[f]

    def chunk(f, carry):
        h = h_ref[...]
        a = conv(jnp.dot(h, wa_ref[f], preferred_element_type=F32), cwa_ref, cba_ref, f)
        b = conv(jnp.dot(h, wb_ref[f], preferred_element_type=F32), cwb_ref, cbb_ref, f)
        act = (_silu(a) * b).astype(BF16)
        acc_ref[...] += jnp.dot(act, wd_ref[f], preferred_element_type=F32)
        return carry

    lax.fori_loop(0, n_chunks, chunk, 0)
    y = x_ref[...] + gate_ref[...] * acc_ref[...]
    if final:
        y = _rms(y, d) * fg_ref[...]
    o_ref[...] = y


def _ffn(x, g, mod, row_of_batch, wa, wb, cwa, cwb, cba, cbb, wd, final_g, final):
    B, T, D = x.shape
    n_chunks, _, fc = wa.shape
    tm = min(T, 512)
    hb = tm // HALO
    vec = lambda sec: pl.BlockSpec((None, 1, D), _mod_index(sec, row_of_batch))
    const3 = lambda a: _resident(a.shape, lambda b, i: (0, 0, 0))
    return pl.pallas_call(
        functools.partial(_ffn_kernel, tm=tm, n_chunks=n_chunks, final=final),
        grid=(B, T // tm),
        in_specs=[
            pl.BlockSpec((None, HALO, D), lambda b, i: (b, jnp.maximum(i * hb - 1, 0), 0)),
            pl.BlockSpec((None, tm, D), lambda b, i: (b, i, 0)),
            pl.BlockSpec((None, HALO, D), lambda b, i: (b, jnp.minimum((i + 1) * hb, T // HALO - 1), 0)),
            pl.BlockSpec((1, D), lambda b, i: (0, 0)),
            vec(3), vec(4), vec(5),
            const3(wa), const3(wb), const3(cwa), const3(cwb), const3(cba), const3(cbb), const3(wd),
            pl.BlockSpec((1, D), lambda b, i: (0, 0)),
        ],
        out_specs=pl.BlockSpec((None, tm, D), lambda b, i: (b, i, 0)),
        out_shape=jax.ShapeDtypeStruct((B, T, D), F32),
        scratch_shapes=[pltpu.VMEM((tm + 2 * HALO, D), BF16), pltpu.VMEM((tm, D), F32)],
        compiler_params=_params("parallel", "parallel"),
        name="conv_ffn",
    )(x, x, x, g.reshape(1, D), mod, mod, mod, wa, wb, cwa, cwb, cba, cbb, wd, final_g.reshape(1, D))


def _swap_halves(w, groups, width):
    lead = w.shape[:-1]
    return jnp.flip(w.reshape(*lead, groups, 2, width // 2), axis=-2).reshape(*lead, groups * width)


def _layout_in_proj(w_in):
    D = w_in.shape[0]
    offs = [0]
    for s in (RET_QK, RET_QK, RET_W, RET_W, MLA_Q_RANK, MLA_KV_RANK, MLA_ROPE, POOL_W):
        offs.append(offs[-1] + s)
    rq, rk, rv, rg, cq, ckv, kr, pool = [w_in[:, offs[j]:offs[j + 1]] for j in range(8)]
    z = lambda n: jnp.zeros((D, n), w_in.dtype)
    place = lambda a: jnp.concatenate([z(MLA_NOPE), a, z(HEAD_PAD - MLA_NOPE - MLA_ROPE)], axis=1)
    cols = [rk, _swap_halves(rk, RET_HEADS, RET_DK), rv, rq, _swap_halves(rq, RET_HEADS, RET_DK), rg,
            cq, z(512 - MLA_Q_RANK), place(kr), place(_swap_halves(kr, 1, MLA_ROPE)), ckv, pool]
    return jnp.concatenate(cols, axis=1).astype(BF16)


def _layout_uq(w_uq):
    r = w_uq.shape[0]
    w = w_uq.reshape(r, MLA_HEADS, MLA_NOPE + MLA_ROPE)
    nope, rope = w[..., :MLA_NOPE], w[..., MLA_NOPE:]
    z = lambda n: jnp.zeros((r, MLA_HEADS, n), w_uq.dtype)
    tail = HEAD_PAD - MLA_NOPE - MLA_ROPE
    per_head = jnp.concatenate([nope, rope, z(tail), z(MLA_NOPE), _swap_halves(rope, 1, MLA_ROPE), z(tail)], axis=-1)
    w = per_head.reshape(r, MLA_HEADS * 2 * HEAD_PAD)
    return jnp.concatenate([w, jnp.zeros((512 - r, w.shape[1]), w.dtype)], axis=0).astype(BF16)


def _layout_ukv(w_ukv):
    r = w_ukv.shape[0]
    w = w_ukv.reshape(r, MLA_HEADS, MLA_NOPE + MLA_DV)
    z = lambda n: jnp.zeros((r, MLA_HEADS, n), w_ukv.dtype)
    per_head = jnp.concatenate([w[..., :MLA_NOPE], z(HEAD_PAD - MLA_NOPE), w[..., MLA_NOPE:], z(HEAD_PAD - MLA_DV)], axis=-1)
    return per_head.reshape(r, MLA_HEADS * 2 * HEAD_PAD).astype(BF16)


def _chunk_cols(w, fc):
    r, f = w.shape
    return w.reshape(r, f // fc, fc).transpose(1, 0, 2)


def _rope_tables(t):
    rows = t // GRID_W
    row = jnp.repeat(jnp.arange(rows, dtype=F32), GRID_W)
    col = jnp.tile(jnp.arange(GRID_W, dtype=F32), rows)
    n_freq = MLA_ROPE // 4
    inv = ROPE_THETA ** (-jnp.arange(n_freq, dtype=F32) / n_freq)
    ang = jnp.concatenate([row[:, None] * inv, col[:, None] * inv], axis=-1)
    cos, sin = jnp.cos(ang), jnp.sin(ang)
    cc = jnp.concatenate([cos, cos], axis=-1)
    ss = jnp.concatenate([-sin, sin], axis=-1)
    ret_cos, ret_sin = jnp.tile(cc, (1, RET_HEADS)), jnp.tile(ss, (1, RET_HEADS))
    tail = HEAD_PAD - MLA_NOPE - MLA_ROPE
    one, zero = jnp.ones((t, MLA_NOPE), F32), jnp.zeros((t, MLA_NOPE), F32)
    mla_cos = jnp.concatenate([one, cc, jnp.ones((t, tail), F32)], axis=-1)
    mla_sin = jnp.concatenate([zero, ss, jnp.zeros((t, tail), F32)], axis=-1)
    return ret_cos, ret_sin, mla_cos, mla_sin


def kernel(x, c, ctx, c_ctx, w_mod, b_mod, norm1_g, w_in, ret_decay_f, ret_decay_b, mla_q_norm_g, w_uq,
           mla_kv_norm_g, w_ukv, pool_w, pool_scale, w_out, norm2_g, w_up, conv_w, conv_b, w_down,
           final_norm_g):
    B, T, D = x.shape
    L = w_mod.shape[0]
    d_ff = w_down.shape[1]
    fc = 256
    assert T % 512 == 0 and ctx.shape[1] % RET_CHUNK == 0 and d_ff % fc == 0 and B + 1 <= SUBLANES

    ret_cos, ret_sin, mla_cos, mla_sin = _rope_tables(T)
    c_rows = jnp.concatenate([c, c_ctx[None, :], jnp.zeros((SUBLANES - B - 1, D), c.dtype)], axis=0)
    mod_all = _modulation(c_rows, w_mod.astype(BF16), b_mod)
    lat_row = lambda b: b
    ctx_row = lambda b: B

    xc = ctx
    zero_state = jnp.zeros((B, RET_QK, RET_W), F32)
    for l in range(L):
        with_ctx_out = l < L - 1
        mod = mod_all[l].reshape(SUBLANES * 6, 1, D)
        w_p = _layout_in_proj(w_in[l])
        wq = _layout_uq(w_uq[l])
        wkv = _layout_ukv(w_ukv[l])
        gq = jnp.concatenate([mla_q_norm_g[l], jnp.zeros((512 - MLA_Q_RANK,), F32)]).reshape(1, 512)
        gkv = mla_kv_norm_g[l].reshape(1, MLA_KV_RANK)
        pool_bd = jax.scipy.linalg.block_diag(*[pool_w[l, g] for g in range(len(POOL_WINDOWS))]).astype(BF16)
        wo = w_out[l].astype(BF16)
        wo_r, wo_m, wo_p = wo[:RET_W], wo[RET_W:RET_W + MLA_W], wo[RET_W + MLA_W:]
        wa, wb = _chunk_cols(w_up[l, :, :d_ff].astype(BF16), fc), _chunk_cols(w_up[l, :, d_ff:].astype(BF16), fc)
        cwa, cwb = _chunk_cols(conv_w[l, :, :d_ff], fc), _chunk_cols(conv_w[l, :, d_ff:], fc)
        cba, cbb = _chunk_cols(conv_b[l, None, :d_ff], fc), _chunk_cols(conv_b[l, None, d_ff:], fc)
        wd = w_down[l].astype(BF16).reshape(d_ff // fc, fc, D)

        p_lat = _in_projection(x, norm1_g[l], mod, lat_row, w_p)
        p_ctx = _in_projection(xc, norm1_g[l], mod, ctx_row, w_p)

        yr_ctx, s_f, s_b = _retention(p_ctx, ret_cos, ret_sin, ret_decay_f[l], ret_decay_b[l], zero_state, zero_state, False)
        yr_lat, _, _ = _retention(p_lat, ret_cos, ret_sin, ret_decay_f[l], ret_decay_b[l], s_f, s_b, True)

        q_ctx, k_ctx, v_ctx = _mla_projection(p_ctx, gq, gkv, wq, wkv, mla_cos, mla_sin, False)
        q_lat, k_lat, v_lat = _mla_projection(p_lat, gq, gkv, wq, wkv, mla_cos, mla_sin, True)
        ym_lat = _attention(q_lat, k_ctx, v_ctx, k_lat, v_lat)

        yp_lat = _pool(p_lat, pool_bd, pool_scale[l])
        x = _out_projection(x, yr_lat, ym_lat, yp_lat, mod, lat_row, wo_r, wo_m, wo_p)
        x = _ffn(x, norm2_g[l], mod, lat_row, wa, wb, cwa, cwb, cba, cbb, wd, final_norm_g, not with_ctx_out)
        if with_ctx_out:
            ym_ctx = _attention(q_ctx, k_ctx, v_ctx)
            yp_ctx = _pool(p_ctx, pool_bd, pool_scale[l])
            xc = _out_projection(xc, yr_ctx, ym_ctx, yp_ctx, mod, ctx_row, wo_r, wo_m, wo_p)
            xc = _ffn(xc, norm2_g[l], mod, ctx_row, wa, wb, cwa, cwb, cba, cbb, wd, final_norm_g, False)
    return x
```

```python
import functools

import jax
import jax.numpy as jnp
from jax import lax
from jax.experimental import pallas as pl
from jax.experimental.pallas import tpu as pltpu

BF16 = jnp.bfloat16
F32 = jnp.float32

GRID_W = 64
ROPE_THETA = 10000.0
NORM_EPS = 1e-6
RET_HEADS = 4
RET_DK = 32
RET_DV = 64
MLA_HEADS = 8
MLA_Q_RANK = 384
MLA_KV_RANK = 256
MLA_NOPE = 64
MLA_ROPE = 32
MLA_DV = 64
POOL_WINDOWS = (2, 4, 8, 16)
POOL_GROUP = 64

LANES = 128
SUBLANES = 8
BF16_ROWS = 16
VMEM_LIMIT_BYTES = 56 * 1024 * 1024

RET_W = RET_HEADS * RET_DV
RET_QK = RET_HEADS * RET_DK
MLA_W = MLA_HEADS * MLA_DV
POOL_W = len(POOL_WINDOWS) * POOL_GROUP
HEAD_PAD = 128
HALO = 16
LOG2_E = 1.4426950408889634
V_ROWS = 256
RET_CHUNK = 128

P_RET = 0
P_CQ = 1024
P_KR = 1536
P_CKV = 1792
P_POOL = 2048
P_WIDTH = 2304


def _params(*semantics):
    return pltpu.CompilerParams(dimension_semantics=semantics, vmem_limit_bytes=VMEM_LIMIT_BYTES)


def _resident(block_shape, index_map):
    return pl.BlockSpec(block_shape, index_map, pipeline_mode=pl.Buffered(1))


def _rms(x, width):
    return x * lax.rsqrt(jnp.sum(x * x, axis=-1, keepdims=True) * (1.0 / width) + NORM_EPS)


def _silu(x):
    return x * jax.nn.sigmoid(x)


def _log_sigmoid(d):
    return jnp.minimum(d, 0.0) - jnp.log1p(jnp.exp(-jnp.abs(d)))


def _mod_kernel(c_ref, w_ref, b_ref, o_ref):
    s = _silu(c_ref[...])
    o_ref[...] = jnp.dot(s.astype(BF16), w_ref[...], preferred_element_type=F32) + b_ref[...]


def _modulation(c_rows, w_mod, b_mod):
    L, D, W = w_mod.shape
    tn = 1536
    return pl.pallas_call(
        _mod_kernel,
        grid=(L, W // tn),
        in_specs=[
            pl.BlockSpec((SUBLANES, D), lambda l, j: (0, 0)),
            pl.BlockSpec((None, D, tn), lambda l, j: (l, 0, j)),
            pl.BlockSpec((None, 1, tn), lambda l, j: (l, 0, j)),
        ],
        out_specs=pl.BlockSpec((None, SUBLANES, tn), lambda l, j: (l, 0, j)),
        out_shape=jax.ShapeDtypeStruct((L, SUBLANES, W), F32),
        compiler_params=_params("parallel", "parallel"),
        name="modulation",
    )(c_rows, w_mod, b_mod.reshape(L, 1, W))


def _inproj_kernel(x_ref, g_ref, sh_ref, sc_ref, w_ref, o_ref):
    x = x_ref[...]
    h = _rms(x, x.shape[-1]) * g_ref[...]
    h = h * (1.0 + sc_ref[...]) + sh_ref[...]
    o_ref[...] = jnp.dot(h.astype(BF16), w_ref[...], preferred_element_type=F32)


def _mod_index(section, row_of_batch):
    def index(b, *_):
        return (row_of_batch(b) * 6 + section, 0, 0)

    return index


def _in_projection(x, g, mod, row_of_batch, w):
    B, T, D = x.shape
    tm = min(T, 512)
    vec = lambda sec: pl.BlockSpec((None, 1, D), _mod_index(sec, row_of_batch))
    return pl.pallas_call(
        _inproj_kernel,
        grid=(B, T // tm),
        in_specs=[
            pl.BlockSpec((None, tm, D), lambda b, i: (b, i, 0)),
            pl.BlockSpec((1, D), lambda b, i: (0, 0)),
            vec(0),
            vec(1),
            _resident((D, P_WIDTH), lambda b, i: (0, 0)),
        ],
        out_specs=pl.BlockSpec((None, tm, P_WIDTH), lambda b, i: (b, i, 0)),
        out_shape=jax.ShapeDtypeStruct((B, T, P_WIDTH), F32),
        compiler_params=_params("parallel", "parallel"),
        name="in_projection",
    )(x, g.reshape(1, D), mod, mod, w)


def _ret_rope(a, a_swap, cos_ref, sin_ref, rotate):
    if not rotate:
        return a
    return a * cos_ref[...] + a_swap * sin_ref[...]


def _ret_state_kernel(pf_ref, pb_ref, cf_ref, sf_ref, cb_ref, sb_ref, dfl_ref, dbl_ref, dfc_ref, dbc_ref,
                      s0f_ref, s0b_ref, sf_all_ref, sb_all_ref, sf_fin_ref, sb_fin_ref, stf_ref, stb_ref,
                      *, rotate):
    c = RET_CHUNK
    i = pl.program_id(1)

    @pl.when(i == 0)
    def _():
        stf_ref[...] = s0f_ref[...]
        stb_ref[...] = s0b_ref[...]

    sf_all_ref[...] = stf_ref[...]
    sb_all_ref[...] = stb_ref[...]

    n = lax.broadcasted_iota(jnp.int32, (c, RET_QK), 0).astype(F32)
    rows = lax.broadcasted_iota(jnp.int32, (RET_QK, RET_W), 0)
    cols = lax.broadcasted_iota(jnp.int32, (RET_QK, RET_W), 1)
    head_block = (rows // RET_DK) == (cols // RET_DV)

    def update(p_ref, cos_ref, sin_ref, lg_lane, lg_col, pos_pow, st_ref):
        k = _ret_rope(p_ref[:, 0:RET_QK], p_ref[:, RET_QK:2 * RET_QK], cos_ref, sin_ref, rotate)
        k = k * (RET_DK ** -0.5)
        v = p_ref[:, 2 * RET_QK:2 * RET_QK + RET_W]
        kd = (k * jnp.exp(lg_lane * pos_pow)).astype(BF16)
        kv = lax.dot_general(kd, v.astype(BF16), (((0,), (0,)), ((), ())), preferred_element_type=F32)
        st_ref[...] = st_ref[...] * jnp.exp(lg_col * float(c)) + jnp.where(head_block, kv, 0.0)

    update(pf_ref, cf_ref, sf_ref, _log_sigmoid(dfl_ref[...]), _log_sigmoid(dfc_ref[...]), (c - 1.0) - n, stf_ref)
    update(pb_ref, cb_ref, sb_ref, _log_sigmoid(dbl_ref[...]), _log_sigmoid(dbc_ref[...]), n, stb_ref)

    @pl.when(i == pl.num_programs(1) - 1)
    def _():
        sf_fin_ref[...] = stf_ref[...]
        sb_fin_ref[...] = stb_ref[...]


def _ret_out_kernel(p_ref, cos_ref, sin_ref, dfl_ref, dbl_ref, dfw_ref, dbw_ref, sf_ref, sb_ref, ones_ref,
                    o_ref, *, rotate):
    c = RET_CHUNK
    k = _ret_rope(p_ref[:, 0:RET_QK], p_ref[:, RET_QK:2 * RET_QK], cos_ref, sin_ref, rotate) * (RET_DK ** -0.5)
    v = p_ref[:, 2 * RET_QK:2 * RET_QK + RET_W]
    q = _ret_rope(p_ref[:, 512:512 + RET_QK], p_ref[:, 512 + RET_QK:512 + 2 * RET_QK], cos_ref, sin_ref, rotate)
    gate = p_ref[:, 768:768 + RET_W]

    kr = lax.broadcasted_iota(jnp.int32, (RET_HEADS * c, RET_QK), 0) // c
    kl = lax.broadcasted_iota(jnp.int32, (RET_HEADS * c, RET_QK), 1) // RET_DK
    k_stack = jnp.where(kr == kl, jnp.tile(k, (RET_HEADS, 1)), 0.0).astype(BF16)
    vr = lax.broadcasted_iota(jnp.int32, (RET_HEADS * c, RET_W), 0) // c
    vl = lax.broadcasted_iota(jnp.int32, (RET_HEADS * c, RET_W), 1) // RET_DV
    v_stack = jnp.where(vr == vl, jnp.tile(v, (RET_HEADS, 1)), 0.0).astype(BF16)

    lgf_w = _log_sigmoid(dfw_ref[...])
    lgb_w = _log_sigmoid(dbw_ref[...])
    nq = lax.broadcasted_iota(jnp.int32, (c, RET_HEADS * c), 0)
    mk = lax.broadcasted_iota(jnp.int32, (c, RET_HEADS * c), 1) % c
    diff = (nq - mk).astype(F32)
    dmat = jnp.exp(jnp.where(diff >= 0.0, lgf_w, -lgb_w) * diff)

    scores = lax.dot_general(q.astype(BF16), k_stack, (((1,), (1,)), ((), ())), preferred_element_type=F32)
    a = (scores * dmat).astype(BF16)

    n = lax.broadcasted_iota(jnp.int32, (c, RET_QK), 0).astype(F32)
    qf = (q * jnp.exp(_log_sigmoid(dfl_ref[...]) * (n + 1.0))).astype(BF16)
    qb = (q * jnp.exp(_log_sigmoid(dbl_ref[...]) * (float(c) - n))).astype(BF16)
    o = jnp.dot(a, v_stack, preferred_element_type=F32)
    o = o + jnp.dot(qf, sf_ref[...].astype(BF16), preferred_element_type=F32)
    o = o + jnp.dot(qb, sb_ref[...].astype(BF16), preferred_element_type=F32)

    sq = o * o
    hi = sq.astype(BF16)
    lo = (sq - hi.astype(F32)).astype(BF16)
    ssum = jnp.dot(hi, ones_ref[...], preferred_element_type=F32) + jnp.dot(lo, ones_ref[...], preferred_element_type=F32)
    o = o * lax.rsqrt(ssum * (1.0 / RET_DV) + NORM_EPS)
    o_ref[...] = (o * _silu(gate)).astype(o_ref.dtype)


def _retention(p, cos, sin, decay_f, decay_b, s0f, s0b, rotate):
    B, T, _ = p.shape
    c = RET_CHUNK
    nc = T // c
    rep = lambda d, r: jnp.repeat(d.astype(F32), r)
    dfl, dbl = rep(decay_f, RET_DK).reshape(1, RET_QK), rep(decay_b, RET_DK).reshape(1, RET_QK)
    dfc, dbc = dfl.reshape(RET_QK, 1), dbl.reshape(RET_QK, 1)
    dfw, dbw = rep(decay_f, c).reshape(1, RET_HEADS * c), rep(decay_b, c).reshape(1, RET_HEADS * c)
    small = lambda shape: pl.BlockSpec(shape, lambda b, i: (0,) * len(shape))
    rope_f = pl.BlockSpec((c, RET_QK), lambda b, i: (i, 0))
    rope_b = pl.BlockSpec((c, RET_QK), lambda b, i: (nc - 1 - i, 0))
    state = pl.BlockSpec((None, RET_QK, RET_W), lambda b, i: (b, 0, 0))

    sf_all, sb_all, sf_fin, sb_fin = pl.pallas_call(
        functools.partial(_ret_state_kernel, rotate=rotate),
        grid=(B, nc),
        in_specs=[
            pl.BlockSpec((None, c, 512), lambda b, i: (b, i, 0)),
            pl.BlockSpec((None, c, 512), lambda b, i: (b, nc - 1 - i, 0)),
            rope_f, rope_f, rope_b, rope_b,
            small((1, RET_QK)), small((1, RET_QK)), small((RET_QK, 1)), small((RET_QK, 1)),
            state, state,
        ],
        out_specs=[
            pl.BlockSpec((None, None, RET_QK, RET_W), lambda b, i: (b, i, 0, 0)),
            pl.BlockSpec((None, None, RET_QK, RET_W), lambda b, i: (b, nc - 1 - i, 0, 0)),
            state, state,
        ],
        out_shape=[
            jax.ShapeDtypeStruct((B, nc, RET_QK, RET_W), F32),
            jax.ShapeDtypeStruct((B, nc, RET_QK, RET_W), F32),
            jax.ShapeDtypeStruct((B, RET_QK, RET_W), F32),
            jax.ShapeDtypeStruct((B, RET_QK, RET_W), F32),
        ],
        scratch_shapes=[pltpu.VMEM((RET_QK, RET_W), F32), pltpu.VMEM((RET_QK, RET_W), F32)],
        compiler_params=_params("parallel", "arbitrary"),
        name="retention_states",
    )(p, p, cos, sin, cos, sin, dfl, dbl, dfc, dbc, s0f, s0b)

    gi = lax.broadcasted_iota(jnp.int32, (RET_W, RET_W), 0) // RET_DV
    gj = lax.broadcasted_iota(jnp.int32, (RET_W, RET_W), 1) // RET_DV
    ones_bd = (gi == gj).astype(BF16)
    chunk_state = pl.BlockSpec((None, None, RET_QK, RET_W), lambda b, i: (b, i, 0, 0))
    y = pl.pallas_call(
        functools.partial(_ret_out_kernel, rotate=rotate),
        grid=(B, nc),
        in_specs=[
            pl.BlockSpec((None, c, 1024), lambda b, i: (b, i, 0)),
            rope_f, rope_f,
            small((1, RET_QK)), small((1, RET_QK)), small((1, RET_HEADS * c)), small((1, RET_HEADS * c)),
            chunk_state, chunk_state,
            small((RET_W, RET_W)),
        ],
        out_specs=pl.BlockSpec((None, c, RET_W), lambda b, i: (b, i, 0)),
        out_shape=jax.ShapeDtypeStruct((B, T, RET_W), BF16),
        compiler_params=_params("parallel", "parallel"),
        name="retention_outputs",
    )(p, cos, sin, dfl, dbl, dfw, dbw, sf_all, sb_all, ones_bd)
    return y, sf_fin, sb_fin


def _pool_kernel(prev_ref, x_ref, next_ref, w_ref, scale_ref, o_ref, *, tp, t_total):
    i = pl.program_id(1)
    x = x_ref[...]
    prev = jnp.where(i > 0, prev_ref[...], 0.0)
    nxt = jnp.where(i < pl.num_programs(1) - 1, next_ref[...], 0.0)
    xp = jnp.concatenate([prev, x, nxt], axis=0)
    rows = tp + 2 * HALO
    back = lambda a, s: pltpu.roll(a, s, axis=0)
    ahead = lambda a, s: pltpu.roll(a, rows - s, axis=0)
    a2 = xp + back(xp, 1)
    a4 = a2 + back(a2, 2)
    a8 = a4 + back(a4, 4)
    a16 = a8 + back(a8, 8)
    lane = lax.broadcasted_iota(jnp.int32, (rows, POOL_W), 1)
    group = lane // POOL_GROUP
    win = jnp.where(group == 0, a2, jnp.where(group == 1, ahead(a4, 1), jnp.where(group == 2, ahead(a8, 3), ahead(a16, 7))))
    win = win[HALO:HALO + tp]

    lane = lax.broadcasted_iota(jnp.int32, (tp, POOL_W), 1)
    half = jnp.left_shift(1, lane // POOL_GROUP)
    t = lax.broadcasted_iota(jnp.int32, (tp, POOL_W), 0) + i * tp
    count = jnp.minimum(t + half, t_total) - jnp.maximum(t - half, 0)
    d = win / count.astype(F32) - x
    y = jnp.dot(d.astype(BF16), w_ref[...], preferred_element_type=F32) * scale_ref[...]
    o_ref[...] = y.astype(o_ref.dtype)


def _pool(p, w_bd, scale):
    B, T, _ = p.shape
    tp = min(T, 1024)
    col = P_POOL // POOL_W
    hb = tp // HALO
    return pl.pallas_call(
        functools.partial(_pool_kernel, tp=tp, t_total=T),
        grid=(B, T // tp),
        in_specs=[
            pl.BlockSpec((None, HALO, POOL_W), lambda b, i: (b, jnp.maximum(i * hb - 1, 0), col)),
            pl.BlockSpec((None, tp, POOL_W), lambda b, i: (b, i, col)),
            pl.BlockSpec((None, HALO, POOL_W), lambda b, i: (b, jnp.minimum((i + 1) * hb, T // HALO - 1), col)),
            pl.BlockSpec((POOL_W, POOL_W), lambda b, i: (0, 0)),
            pl.BlockSpec((1, POOL_W), lambda b, i: (0, 0)),
        ],
        out_specs=pl.BlockSpec((None, tp, POOL_W), lambda b, i: (b, i, 0)),
        out_shape=jax.ShapeDtypeStruct((B, T, POOL_W), BF16),
        compiler_params=_params("parallel", "parallel"),
        name="pooling",
    )(p, p, p, w_bd, scale.reshape(1, POOL_W))


def _mla_proj_kernel(cq_ref, kr_ref, ckv_ref, gq_ref, gkv_ref, wq_ref, wkv_ref, cos_ref, sin_ref,
                     q_ref, k_ref, v_ref, *, rotate):
    scale = (MLA_NOPE + MLA_ROPE) ** -0.5 * LOG2_E
    cq = _rms(cq_ref[...], MLA_Q_RANK) * gq_ref[...]
    q = jnp.dot(cq.astype(BF16), wq_ref[...], preferred_element_type=F32)
    ckv = _rms(ckv_ref[...], MLA_KV_RANK) * gkv_ref[...]
    kv = jnp.dot(ckv.astype(BF16), wkv_ref[...], preferred_element_type=F32)
    kr = kr_ref[:, 0:HEAD_PAD]
    if rotate:
        kr = kr * cos_ref[...] + kr_ref[:, HEAD_PAD:2 * HEAD_PAD] * sin_ref[...]
    ones_col = (lax.broadcasted_iota(jnp.int32, (1, HEAD_PAD), 1) == MLA_DV).astype(F32)
    left = (pl.program_id(1) % 2 == 0).astype(F32)
    for h in range(MLA_HEADS):
        base = 2 * HEAD_PAD * h
        qh = q[:, base:base + HEAD_PAD]
        if rotate:
            qh = qh * cos_ref[...] + q[:, base + HEAD_PAD:base + 2 * HEAD_PAD] * sin_ref[...]
        q_ref[h] = (qh * scale).astype(BF16)
        k_ref[h] = (kv[:, base:base + HEAD_PAD] + kr).astype(BF16)
        vh = kv[:, base + HEAD_PAD:base + 2 * HEAD_PAD] + ones_col
        v_ref[h] = jnp.concatenate([vh * left, vh * (1.0 - left)], axis=-1).astype(BF16)


def _mla_projection(p, gq, gkv, wq, wkv, cos, sin, rotate):
    B, T, _ = p.shape
    tm = V_ROWS
    heads = pl.BlockSpec((None, MLA_HEADS, tm, HEAD_PAD), lambda b, i: (b, 0, i, 0))
    shape = jax.ShapeDtypeStruct((B, MLA_HEADS, T, HEAD_PAD), BF16)
    v_heads = pl.BlockSpec((None, MLA_HEADS, tm, 2 * HEAD_PAD), lambda b, i: (b, 0, i, 0))
    v_shape = jax.ShapeDtypeStruct((B, MLA_HEADS, T, 2 * HEAD_PAD), BF16)
    const = lambda shape_: _resident(shape_, lambda b, i: (0, 0))
    return pl.pallas_call(
        functools.partial(_mla_proj_kernel, rotate=rotate),
        grid=(B, T // tm),
        in_specs=[
            pl.BlockSpec((None, tm, 512), lambda b, i: (b, i, P_CQ // 512)),
            pl.BlockSpec((None, tm, 256), lambda b, i: (b, i, P_KR // 256)),
            pl.BlockSpec((None, tm, 256), lambda b, i: (b, i, P_CKV // 256)),
            const((1, 512)), const((1, 256)), const((512, 2 * HEAD_PAD * MLA_HEADS)),
            const((256, 2 * HEAD_PAD * MLA_HEADS)),
            pl.BlockSpec((tm, HEAD_PAD), lambda b, i: (i, 0)),
            pl.BlockSpec((tm, HEAD_PAD), lambda b, i: (i, 0)),
        ],
        out_specs=[heads, heads, v_heads],
        out_shape=[shape, shape, v_shape],
        compiler_params=_params("parallel", "parallel"),
        name="mla_projection",
    )(p, p, p, gq, gkv, wq, wkv, cos, sin)


def _attn_kernel(q_ref, kc_ref, vc_ref, *rest, tk, n_lat, heads):
    if n_lat:
        kl_ref, vl_ref, o_ref, m_ref, acc_ref, s_ref = rest
    else:
        o_ref, m_ref, acc_ref = rest
    m_ref[...] = jnp.full_like(m_ref, -1e30)
    acc_ref[...] = jnp.zeros_like(acc_ref)

    def scores(h, k):
        return lax.dot_general(q_ref[h], k, (((1,), (1,)), ((), ())), preferred_element_type=F32)

    def update(h, s, v):
        m_old = m_ref[h]
        m_new = jnp.maximum(m_old[:, 0:1], jnp.max(s, axis=-1, keepdims=True))
        p = jnp.exp2(s - m_new)
        alpha = jnp.exp2(m_old - m_new)
        pv = jnp.dot(p.astype(BF16), v, preferred_element_type=F32)
        acc_ref[h] = jnp.concatenate([alpha, alpha], axis=-1) * acc_ref[h] + pv
        m_ref[h] = jnp.broadcast_to(m_new, m_old.shape)

    def lat(ref, h, j):
        return ref[h, pl.ds(pl.multiple_of(j * tk, tk), tk), :]

    def half(j, cur, nxt):
        for h in range(heads):
            if nxt is not None:
                s_ref[nxt, h] = scores(h, lat(kl_ref, h, j + 1))
            update(h, s_ref[cur, h], lat(vl_ref, h, j))

    ctx_scores = [scores(h, kc_ref[h]) for h in range(heads)]
    if n_lat:
        for h in range(heads):
            s_ref[0, h] = scores(h, lat(kl_ref, h, 0))
    for h in range(heads):
        update(h, ctx_scores[h], vc_ref[h])
    if n_lat:
        def body(jj, carry):
            half(2 * jj, 0, 1)
            half(2 * jj + 1, 1, 0)
            return carry

        lax.fori_loop(0, n_lat // 2 - 1, body, 0)
        half(n_lat - 2, 0, 1)
        half(n_lat - 1, 1, None)
    outs = []
    for h in range(heads):
        acc = acc_ref[h][:, 0:HEAD_PAD] + acc_ref[h][:, HEAD_PAD:2 * HEAD_PAD]
        outs.append(acc[:, 0:MLA_DV] / acc[:, MLA_DV:MLA_DV + 1])
    o_ref[...] = jnp.concatenate(outs, axis=-1).astype(o_ref.dtype)


def _attention(q, k_ctx, v_ctx, k_lat=None, v_lat=None):
    B, H, Tq, _ = q.shape
    Tc = k_ctx.shape[2]
    heads = 2
    tq = min(Tq, 1024)
    tk = 512
    n_lat = 0 if k_lat is None else k_lat.shape[2] // tk
    assert tk % (2 * V_ROWS) == 0 and Tc % V_ROWS == 0
    whole = lambda t, w: pl.BlockSpec((None, heads, t, w), lambda b, g, i: (b, g, 0, 0))
    in_specs = [pl.BlockSpec((None, heads, tq, HEAD_PAD), lambda b, g, i: (b, g, i, 0)),
                whole(Tc, HEAD_PAD), whole(Tc, 2 * HEAD_PAD)]
    args = [q, k_ctx, v_ctx]
    scratch = [pltpu.VMEM((heads, tq, HEAD_PAD), F32), pltpu.VMEM((heads, tq, 2 * HEAD_PAD), F32)]
    if n_lat:
        assert n_lat % 2 == 0
        in_specs += [whole(k_lat.shape[2], HEAD_PAD), whole(k_lat.shape[2], 2 * HEAD_PAD)]
        args += [k_lat, v_lat]
        scratch.append(pltpu.VMEM((2, heads, tq, tk), F32))
    return pl.pallas_call(
        functools.partial(_attn_kernel, tk=tk, n_lat=n_lat, heads=heads),
        grid=(B, H // heads, Tq // tq),
        in_specs=in_specs,
        out_specs=pl.BlockSpec((None, tq, heads * MLA_DV), lambda b, g, i: (b, i, g)),
        out_shape=jax.ShapeDtypeStruct((B, Tq, H * MLA_DV), BF16),
        scratch_shapes=scratch,
        compiler_params=_params("parallel", "parallel", "parallel"),
        name="attention",
    )(*args)


def _outproj_kernel(x_ref, yr_ref, ym_ref, yp_ref, gate_ref, wr_ref, wm_ref, wp_ref, o_ref):
    y = jnp.dot(yr_ref[...], wr_ref[...], preferred_element_type=F32)
    y = y + jnp.dot(ym_ref[...], wm_ref[...], preferred_element_type=F32)
    y = y + jnp.dot(yp_ref[...], wp_ref[...], preferred_element_type=F32)
    o_ref[...] = x_ref[...] + gate_ref[...] * y


def _out_projection(x, y_ret, y_mla, y_pool, mod, row_of_batch, w_r, w_m, w_p):
    B, T, D = x.shape
    tm = min(T, 512)
    rows = lambda w: pl.BlockSpec((None, tm, w), lambda b, i: (b, i, 0))
    const = lambda shape: _resident(shape, lambda b, i: (0, 0))
    return pl.pallas_call(
        _outproj_kernel,
        grid=(B, T // tm),
        in_specs=[
            rows(D), rows(RET_W), rows(MLA_W), rows(POOL_W),
            pl.BlockSpec((None, 1, D), _mod_index(2, row_of_batch)),
            const((RET_W, D)), const((MLA_W, D)), const((POOL_W, D)),
        ],
        out_specs=rows(D),
        out_shape=jax.ShapeDtypeStruct((B, T, D), F32),
        compiler_params=_params("parallel", "parallel"),
        name="out_projection",
    )(x, y_ret, y_mla, y_pool, mod, w_r, w_m, w_p)


def _ffn_kernel(xp_ref, x_ref, xn_ref, g_ref, sh_ref, sc_ref, gate_ref, wa_ref, wb_ref, cwa_ref, cwb_ref,
                cba_ref, cbb_ref, wd_ref, fg_ref, o_ref, h_ref, acc_ref, *, tm, n_chunks, final):
    i = pl.program_id(1)
    d = x_ref.shape[-1]

    def norm_mod(x):
        h = _rms(x, d) * g_ref[...]
        return h * (1.0 + sc_ref[...]) + sh_ref[...]

    h_ref[0:HALO, :] = jnp.where(i > 0, norm_mod(xp_ref[...]), 0.0).astype(BF16)
    h_ref[HALO:HALO + tm, :] = norm_mod(x_ref[...]).astype(BF16)
    h_ref[HALO + tm:2 * HALO + tm, :] = jnp.where(i < pl.num_programs(1) - 1, norm_mod(xn_ref[...]), 0.0).astype(BF16)
    acc_ref[...] = jnp.zeros_like(acc_ref)
    rows = tm + 2 * HALO

    def conv(u, cw_ref, cb_ref, f):
        cw = cw_ref[f]
        y = pltpu.roll(u, 1, axis=0) * cw[0:1] + u * cw[1:2] + pltpu.roll(u, rows - 1, axis=0) * cw[2:3]
        return y[HALO:HALO + tm] + cb_ref[f]

    def chunk(f, carry):
        h = h_ref[...]
        a = conv(jnp.dot(h, wa_ref[f], preferred_element_type=F32), cwa_ref, cba_ref, f)
        b = conv(jnp.dot(h, wb_ref[f], preferred_element_type=F32), cwb_ref, cbb_ref, f)
        act = (_silu(a) * b).astype(BF16)
        acc_ref[...] += jnp.dot(act, wd_ref[f], preferred_element_type=F32)
        return carry

    lax.fori_loop(0, n_chunks, chunk, 0)
    y = x_ref[...] + gate_ref[...] * acc_ref[...]
    if final:
        y = _rms(y, d) * fg_ref[...]
    o_ref[...] = y


def _ffn(x, g, mod, row_of_batch, wa, wb, cwa, cwb, cba, cbb, wd, final_g, final):
    B, T, D = x.shape
    n_chunks, _, fc = wa.shape
    tm = min(T, 512)
    hb = tm // HALO
    vec = lambda sec: pl.BlockSpec((None, 1, D), _mod_index(sec, row_of_batch))
    const3 = lambda a: _resident(a.shape, lambda b, i: (0, 0, 0))
    return pl.pallas_call(
        functools.partial(_ffn_kernel, tm=tm, n_chunks=n_chunks, final=final),
        grid=(B, T // tm),
        in_specs=[
            pl.BlockSpec((None, HALO, D), lambda b, i: (b, jnp.maximum(i * hb - 1, 0), 0)),
            pl.BlockSpec((None, tm, D), lambda b, i: (b, i, 0)),
            pl.BlockSpec((None, HALO, D), lambda b, i: (b, jnp.minimum((i + 1) * hb, T // HALO - 1), 0)),
            pl.BlockSpec((1, D), lambda b, i: (0, 0)),
            vec(3), vec(4), vec(5),
            const3(wa), const3(wb), const3(cwa), const3(cwb), const3(cba), const3(cbb), const3(wd),
            pl.BlockSpec((1, D), lambda b, i: (0, 0)),
        ],
        out_specs=pl.BlockSpec((None, tm, D), lambda b, i: (b, i, 0)),
        out_shape=jax.ShapeDtypeStruct((B, T, D), F32),
        scratch_shapes=[pltpu.VMEM((tm + 2 * HALO, D), BF16), pltpu.VMEM((tm, D), F32)],
        compiler_params=_params("parallel", "parallel"),
        name="conv_ffn",
    )(x, x, x, g.reshape(1, D), mod, mod, mod, wa, wb, cwa, cwb, cba, cbb, wd, final_g.reshape(1, D))


def _swap_halves(w, groups, width):
    lead = w.shape[:-1]
    return jnp.flip(w.reshape(*lead, groups, 2, width // 2), axis=-2).reshape(*lead, groups * width)


def _layout_in_proj(w_in):
    D = w_in.shape[0]
    offs = [0]
    for s in (RET_QK, RET_QK, RET_W, RET_W, MLA_Q_RANK, MLA_KV_RANK, MLA_ROPE, POOL_W):
        offs.append(offs[-1] + s)
    rq, rk, rv, rg, cq, ckv, kr, pool = [w_in[:, offs[j]:offs[j + 1]] for j in range(8)]
    z = lambda n: jnp.zeros((D, n), w_in.dtype)
    place = lambda a: jnp.concatenate([z(MLA_NOPE), a, z(HEAD_PAD - MLA_NOPE - MLA_ROPE)], axis=1)
    cols = [rk, _swap_halves(rk, RET_HEADS, RET_DK), rv, rq, _swap_halves(rq, RET_HEADS, RET_DK), rg,
            cq, z(512 - MLA_Q_RANK), place(kr), place(_swap_halves(kr, 1, MLA_ROPE)), ckv, pool]
    return jnp.concatenate(cols, axis=1).astype(BF16)


def _layout_uq(w_uq):
    r = w_uq.shape[0]
    w = w_uq.reshape(r, MLA_HEADS, MLA_NOPE + MLA_ROPE)
    nope, rope = w[..., :MLA_NOPE], w[..., MLA_NOPE:]
    z = lambda n: jnp.zeros((r, MLA_HEADS, n), w_uq.dtype)
    tail = HEAD_PAD - MLA_NOPE - MLA_ROPE
    per_head = jnp.concatenate([nope, rope, z(tail), z(MLA_NOPE), _swap_halves(rope, 1, MLA_ROPE), z(tail)], axis=-1)
    w = per_head.reshape(r, MLA_HEADS * 2 * HEAD_PAD)
    return jnp.concatenate([w, jnp.zeros((512 - r, w.shape[1]), w.dtype)], axis=0).astype(BF16)


def _layout_ukv(w_ukv):
    r = w_ukv.shape[0]
    w = w_ukv.reshape(r, MLA_HEADS, MLA_NOPE + MLA_DV)
    z = lambda n: jnp.zeros((r, MLA_HEADS, n), w_ukv.dtype)
    per_head = jnp.concatenate([w[..., :MLA_NOPE], z(HEAD_PAD - MLA_NOPE), w[..., MLA_NOPE:], z(HEAD_PAD - MLA_DV)], axis=-1)
    return per_head.reshape(r, MLA_HEADS * 2 * HEAD_PAD).astype(BF16)


def _chunk_cols(w, fc):
    r, f = w.shape
    return w.reshape(r, f // fc, fc).transpose(1, 0, 2)


def _rope_tables(t):
    rows = t // GRID_W
    row = jnp.repeat(jnp.arange(rows, dtype=F32), GRID_W)
    col = jnp.tile(jnp.arange(GRID_W, dtype=F32), rows)
    n_freq = MLA_ROPE // 4
    inv = ROPE_THETA ** (-jnp.arange(n_freq, dtype=F32) / n_freq)
    ang = jnp.concatenate([row[:, None] * inv, col[:, None] * inv], axis=-1)
    cos, sin = jnp.cos(ang), jnp.sin(ang)
    cc = jnp.concatenate([cos, cos], axis=-1)
    ss = jnp.concatenate([-sin, sin], axis=-1)
    ret_cos, ret_sin = jnp.tile(cc, (1, RET_HEADS)), jnp.tile(ss, (1, RET_HEADS))
    tail = HEAD_PAD - MLA_NOPE - MLA_ROPE
    one, zero = jnp.ones((t, MLA_NOPE), F32), jnp.zeros((t, MLA_NOPE), F32)
    mla_cos = jnp.concatenate([one, cc, jnp.ones((t, tail), F32)], axis=-1)
    mla_sin = jnp.concatenate([zero, ss, jnp.zeros((t, tail), F32)], axis=-1)
    return ret_cos, ret_sin, mla_cos, mla_sin


def kernel(x, c, ctx, c_ctx, w_mod, b_mod, norm1_g, w_in, ret_decay_f, ret_decay_b, mla_q_norm_g, w_uq,
           mla_kv_norm_g, w_ukv, pool_w, pool_scale, w_out, norm2_g, w_up, conv_w, conv_b, w_down,
           final_norm_g):
    B, T, D = x.shape
    L = w_mod.shape[0]
    d_ff = w_down.shape[1]
    fc = 256
    assert T % 512 == 0 and ctx.shape[1] % RET_CHUNK == 0 and d_ff % fc == 0 and B + 1 <= SUBLANES

    ret_cos, ret_sin, mla_cos, mla_sin = _rope_tables(T)
    c_rows = jnp.concatenate([c, c_ctx[None, :], jnp.zeros((SUBLANES - B - 1, D), c.dtype)], axis=0)
    mod_all = _modulation(c_rows, w_mod.astype(BF16), b_mod)
    lat_row = lambda b: b
    ctx_row = lambda b: B

    xc = ctx
    zero_state = jnp.zeros((B, RET_QK, RET_W), F32)
    for l in range(L):
        with_ctx_out = l < L - 1
        mod = mod_all[l].reshape(SUBLANES * 6, 1, D)
        w_p = _layout_in_proj(w_in[l])
        wq = _layout_uq(w_uq[l])
        wkv = _layout_ukv(w_ukv[l])
        gq = jnp.concatenate([mla_q_norm_g[l], jnp.zeros((512 - MLA_Q_RANK,), F32)]).reshape(1, 512)
        gkv = mla_kv_norm_g[l].reshape(1, MLA_KV_RANK)
        pool_bd = jax.scipy.linalg.block_diag(*[pool_w[l, g] for g in range(len(POOL_WINDOWS))]).astype(BF16)
        wo = w_out[l].astype(BF16)
        wo_r, wo_m, wo_p = wo[:RET_W], wo[RET_W:RET_W + MLA_W], wo[RET_W + MLA_W:]
        wa, wb = _chunk_cols(w_up[l, :, :d_ff].astype(BF16), fc), _chunk_cols(w_up[l, :, d_ff:].astype(BF16), fc)
        cwa, cwb = _chunk_cols(conv_w[l, :, :d_ff], fc), _chunk_cols(conv_w[l, :, d_ff:], fc)
        cba, cbb = _chunk_cols(conv_b[l, None, :d_ff], fc), _chunk_cols(conv_b[l, None, d_ff:], fc)
        wd = w_down[l].astype(BF16).reshape(d_ff // fc, fc, D)

        p_lat = _in_projection(x, norm1_g[l], mod, lat_row, w_p)
        p_ctx = _in_projection(xc, norm1_g[l], mod, ctx_row, w_p)

        yr_ctx, s_f, s_b = _retention(p_ctx, ret_cos, ret_sin, ret_decay_f[l], ret_decay_b[l], zero_state, zero_state, False)
        yr_lat, _, _ = _retention(p_lat, ret_cos, ret_sin, ret_decay_f[l], ret_decay_b[l], s_f, s_b, True)

        q_ctx, k_ctx, v_ctx = _mla_projection(p_ctx, gq, gkv, wq, wkv, mla_cos, mla_sin, False)
        q_lat, k_lat, v_lat = _mla_projection(p_lat, gq, gkv, wq, wkv, mla_cos, mla_sin, True)
        ym_lat = _attention(q_lat, k_ctx, v_ctx, k_lat, v_lat)

        yp_lat = _pool(p_lat, pool_bd, pool_scale[l])
        x = _out_projection(x, yr_lat, ym_lat, yp_lat, mod, lat_row, wo_r, wo_m, wo_p)
        x = _ffn(x, norm2_g[l], mod, lat_row, wa, wb, cwa, cwb, cba, cbb, wd, final_norm_g, not with_ctx_out)
        if with_ctx_out:
            ym_ctx = _attention(q_ctx, k_ctx, v_ctx)
            yp_ctx = _pool(p_ctx, pool_bd, pool_scale[l])
            xc = _out_projection(xc, yr_ctx, ym_ctx, yp_ctx, mod, ctx_row, wo_r, wo_m, wo_p)
            xc = _ffn(xc, norm2_g[l], mod, ctx_row, wa, wb, cwa, cwb, cba, cbb, wd, final_norm_g, False)
    return x
```

```python
import functools

import jax
import jax.numpy as jnp
from jax import lax
from jax.experimental import pallas as pl
from jax.experimental.pallas import tpu as pltpu

BF16 = jnp.bfloat16
F32 = jnp.float32

GRID_W = 64
ROPE_THETA = 10000.0
NORM_EPS = 1e-6
RET_HEADS = 4
RET_DK = 32
RET_DV = 64
MLA_HEADS = 8
MLA_Q_RANK = 384
MLA_KV_RANK = 256
MLA_NOPE = 64
MLA_ROPE = 32
MLA_DV = 64
POOL_WINDOWS = (2, 4, 8, 16)
POOL_GROUP = 64

LANES = 128
SUBLANES = 8
BF16_ROWS = 16
VMEM_LIMIT_BYTES = 56 * 1024 * 1024

RET_W = RET_HEADS * RET_DV
RET_QK = RET_HEADS * RET_DK
MLA_W = MLA_HEADS * MLA_DV
POOL_W = len(POOL_WINDOWS) * POOL_GROUP
HEAD_PAD = 128
HALO = 16
LOG2_E = 1.4426950408889634
V_ROWS = 256
RET_CHUNK = 128

P_RET = 0
P_CQ = 1024
P_KR = 1536
P_CKV = 1792
P_POOL = 2048
P_WIDTH = 2304


def _params(*semantics):
    return pltpu.CompilerParams(dimension_semantics=semantics, vmem_limit_bytes=VMEM_LIMIT_BYTES)


def _resident(block_shape, index_map):
    return pl.BlockSpec(block_shape, index_map, pipeline_mode=pl.Buffered(1))


def _rms(x, width):
    return x * lax.rsqrt(jnp.sum(x * x, axis=-1, keepdims=True) * (1.0 / width) + NORM_EPS)


def _silu(x):
    return x * jax.nn.sigmoid(x)


def _log_sigmoid(d):
    return jnp.minimum(d, 0.0) - jnp.log1p(jnp.exp(-jnp.abs(d)))


def _mod_kernel(c_ref, w_ref, b_ref, o_ref):
    s = _silu(c_ref[...])
    o_ref[...] = jnp.dot(s.astype(BF16), w_ref[...], preferred_element_type=F32) + b_ref[...]


def _modulation(c_rows, w_mod, b_mod):
    L, D, W = w_mod.shape
    tn = 1536
    return pl.pallas_call(
        _mod_kernel,
        grid=(L, W // tn),
        in_specs=[
            pl.BlockSpec((SUBLANES, D), lambda l, j: (0, 0)),
            pl.BlockSpec((None, D, tn), lambda l, j: (l, 0, j)),
            pl.BlockSpec((None, 1, tn), lambda l, j: (l, 0, j)),
        ],
        out_specs=pl.BlockSpec((None, SUBLANES, tn), lambda l, j: (l, 0, j)),
        out_shape=jax.ShapeDtypeStruct((L, SUBLANES, W), F32),
        compiler_params=_params("parallel", "parallel"),
        name="modulation",
    )(c_rows, w_mod, b_mod.reshape(L, 1, W))


def _inproj_kernel(x_ref, g_ref, sh_ref, sc_ref, w_ref, o_ref):
    x = x_ref[...]
    h = _rms(x, x.shape[-1]) * g_ref[...]
    h = h * (1.0 + sc_ref[...]) + sh_ref[...]
    o_ref[...] = jnp.dot(h.astype(BF16), w_ref[...], preferred_element_type=F32)


def _mod_index(section, row_of_batch):
    def index(b, *_):
        return (row_of_batch(b) * 6 + section, 0, 0)

    return index


def _in_projection(x, g, mod, row_of_batch, w):
    B, T, D = x.shape
    tm = min(T, 512)
    vec = lambda sec: pl.BlockSpec((None, 1, D), _mod_index(sec, row_of_batch))
    return pl.pallas_call(
        _inproj_kernel,
        grid=(B, T // tm),
        in_specs=[
            pl.BlockSpec((None, tm, D), lambda b, i: (b, i, 0)),
            pl.BlockSpec((1, D), lambda b, i: (0, 0)),
            vec(0),
            vec(1),
            _resident((D, P_WIDTH), lambda b, i: (0, 0)),
        ],
        out_specs=pl.BlockSpec((None, tm, P_WIDTH), lambda b, i: (b, i, 0)),
        out_shape=jax.ShapeDtypeStruct((B, T, P_WIDTH), F32),
        compiler_params=_params("parallel", "parallel"),
        name="in_projection",
    )(x, g.reshape(1, D), mod, mod, w)


def _ret_rope(a, a_swap, cos, sin, rotate):
    if not rotate:
        return a
    return a * cos + a_swap * sin


def _ret_state_kernel(pf_ref, pb_ref, cf_ref, sf_ref, cb_ref, sb_ref, dfl_ref, dbl_ref, dfc_ref, dbc_ref,
                      s0f_ref, s0b_ref, sf_all_ref, sb_all_ref, sf_fin_ref, sb_fin_ref, stf_ref, stb_ref,
                      *, rotate):
    c = RET_CHUNK
    i = pl.program_id(0)

    @pl.when(i == 0)
    def _():
        stf_ref[...] = s0f_ref[...]
        stb_ref[...] = s0b_ref[...]

    sf_all_ref[...] = stf_ref[...]
    sb_all_ref[...] = stb_ref[...]

    n = lax.broadcasted_iota(jnp.int32, (c, RET_QK), 0).astype(F32)
    rows = lax.broadcasted_iota(jnp.int32, (RET_QK, RET_W), 0)
    cols = lax.broadcasted_iota(jnp.int32, (RET_QK, RET_W), 1)
    head_block = (rows // RET_DK) == (cols // RET_DV)

    def update(p_ref, cos_ref, sin_ref, lg_lane, lg_col, pos_pow, st_ref):
        k_decay = jnp.exp(lg_lane * pos_pow) * (RET_DK ** -0.5)
        carry_decay = jnp.exp(lg_col * float(c))
        for b in range(p_ref.shape[0]):
            k = _ret_rope(p_ref[b, :, 0:RET_QK], p_ref[b, :, RET_QK:2 * RET_QK], cos_ref[...], sin_ref[...], rotate)
            v = p_ref[b, :, 2 * RET_QK:2 * RET_QK + RET_W]
            kv = lax.dot_general((k * k_decay).astype(BF16), v.astype(BF16), (((0,), (0,)), ((), ())),
                                 preferred_element_type=F32)
            st_ref[b] = st_ref[b] * carry_decay + jnp.where(head_block, kv, 0.0)

    update(pf_ref, cf_ref, sf_ref, _log_sigmoid(dfl_ref[...]), _log_sigmoid(dfc_ref[...]), (c - 1.0) - n, stf_ref)
    update(pb_ref, cb_ref, sb_ref, _log_sigmoid(dbl_ref[...]), _log_sigmoid(dbc_ref[...]), n, stb_ref)

    @pl.when(i == pl.num_programs(0) - 1)
    def _():
        sf_fin_ref[...] = stf_ref[...]
        sb_fin_ref[...] = stb_ref[...]


def _ret_out_kernel(p_ref, cos_ref, sin_ref, dfl_ref, dbl_ref, dfw_ref, dbw_ref, sf_ref, sb_ref, ones_ref,
                    o_ref, *, rotate, n_sub):
    c = RET_CHUNK
    kr = lax.broadcasted_iota(jnp.int32, (RET_HEADS * c, RET_QK), 0) // c
    kl = lax.broadcasted_iota(jnp.int32, (RET_HEADS * c, RET_QK), 1) // RET_DK
    vr = lax.broadcasted_iota(jnp.int32, (RET_HEADS * c, RET_W), 0) // c
    vl = lax.broadcasted_iota(jnp.int32, (RET_HEADS * c, RET_W), 1) // RET_DV

    lgf_w = _log_sigmoid(dfw_ref[...])
    lgb_w = _log_sigmoid(dbw_ref[...])
    nq = lax.broadcasted_iota(jnp.int32, (c, RET_HEADS * c), 0)
    mk = lax.broadcasted_iota(jnp.int32, (c, RET_HEADS * c), 1) % c
    diff = (nq - mk).astype(F32)
    dmat = jnp.exp(jnp.where(diff >= 0.0, lgf_w, -lgb_w) * diff)
    n = lax.broadcasted_iota(jnp.int32, (c, RET_QK), 0).astype(F32)
    qf_decay = jnp.exp(_log_sigmoid(dfl_ref[...]) * (n + 1.0))
    qb_decay = jnp.exp(_log_sigmoid(dbl_ref[...]) * (float(c) - n))

    for s in range(n_sub):
        r = slice(s * c, (s + 1) * c)
        cos, sin = cos_ref[r, :], sin_ref[r, :]
        k = _ret_rope(p_ref[r, 0:RET_QK], p_ref[r, RET_QK:2 * RET_QK], cos, sin, rotate) * (RET_DK ** -0.5)
        v = p_ref[r, 2 * RET_QK:2 * RET_QK + RET_W]
        q = _ret_rope(p_ref[r, 512:512 + RET_QK], p_ref[r, 512 + RET_QK:512 + 2 * RET_QK], cos, sin, rotate)
        gate = p_ref[r, 768:768 + RET_W]
        k_stack = jnp.where(kr == kl, jnp.tile(k, (RET_HEADS, 1)), 0.0).astype(BF16)
        v_stack = jnp.where(vr == vl, jnp.tile(v, (RET_HEADS, 1)), 0.0).astype(BF16)

        scores = lax.dot_general(q.astype(BF16), k_stack, (((1,), (1,)), ((), ())), preferred_element_type=F32)
        a = (scores * dmat).astype(BF16)
        o = jnp.dot(a, v_stack, preferred_element_type=F32)
        o = o + jnp.dot((q * qf_decay).astype(BF16), sf_ref[s].astype(BF16), preferred_element_type=F32)
        o = o + jnp.dot((q * qb_decay).astype(BF16), sb_ref[s].astype(BF16), preferred_element_type=F32)

        sq = o * o
        hi = sq.astype(BF16)
        lo = (sq - hi.astype(F32)).astype(BF16)
        ssum = (jnp.dot(hi, ones_ref[...], preferred_element_type=F32)
                + jnp.dot(lo, ones_ref[...], preferred_element_type=F32))
        o = o * lax.rsqrt(ssum * (1.0 / RET_DV) + NORM_EPS)
        o_ref[r, :] = (o * _silu(gate)).astype(o_ref.dtype)


def _retention(p, cos, sin, decay_f, decay_b, s0f, s0b, rotate):
    B, T, _ = p.shape
    c = RET_CHUNK
    nc = T // c
    rep = lambda d, r: jnp.repeat(d.astype(F32), r)
    dfl, dbl = rep(decay_f, RET_DK).reshape(1, RET_QK), rep(decay_b, RET_DK).reshape(1, RET_QK)
    dfc, dbc = dfl.reshape(RET_QK, 1), dbl.reshape(RET_QK, 1)
    dfw, dbw = rep(decay_f, c).reshape(1, RET_HEADS * c), rep(decay_b, c).reshape(1, RET_HEADS * c)
    small1 = lambda shape: pl.BlockSpec(shape, lambda i: (0,) * len(shape))
    rope_f = pl.BlockSpec((c, RET_QK), lambda i: (i, 0))
    rope_b = pl.BlockSpec((c, RET_QK), lambda i: (nc - 1 - i, 0))
    state = pl.BlockSpec((B, RET_QK, RET_W), lambda i: (0, 0, 0))

    sf_all, sb_all, sf_fin, sb_fin = pl.pallas_call(
        functools.partial(_ret_state_kernel, rotate=rotate),
        grid=(nc,),
        in_specs=[
            pl.BlockSpec((B, c, 512), lambda i: (0, i, 0)),
            pl.BlockSpec((B, c, 512), lambda i: (0, nc - 1 - i, 0)),
            rope_f, rope_f, rope_b, rope_b,
            small1((1, RET_QK)), small1((1, RET_QK)), small1((RET_QK, 1)), small1((RET_QK, 1)),
            state, state,
        ],
        out_specs=[
            pl.BlockSpec((B, None, RET_QK, RET_W), lambda i: (0, i, 0, 0)),
            pl.BlockSpec((B, None, RET_QK, RET_W), lambda i: (0, nc - 1 - i, 0, 0)),
            state, state,
        ],
        out_shape=[
            jax.ShapeDtypeStruct((B, nc, RET_QK, RET_W), F32),
            jax.ShapeDtypeStruct((B, nc, RET_QK, RET_W), F32),
            jax.ShapeDtypeStruct((B, RET_QK, RET_W), F32),
            jax.ShapeDtypeStruct((B, RET_QK, RET_W), F32),
        ],
        scratch_shapes=[pltpu.VMEM((B, RET_QK, RET_W), F32), pltpu.VMEM((B, RET_QK, RET_W), F32)],
        compiler_params=_params("arbitrary"),
        name="retention_states",
    )(p, p, cos, sin, cos, sin, dfl, dbl, dfc, dbc, s0f, s0b)

    gi = lax.broadcasted_iota(jnp.int32, (RET_W, RET_W), 0) // RET_DV
    gj = lax.broadcasted_iota(jnp.int32, (RET_W, RET_W), 1) // RET_DV
    ones_bd = (gi == gj).astype(BF16)
    n_sub = min(nc, 4)
    tm = n_sub * c
    small = lambda shape: pl.BlockSpec(shape, lambda b, i: (0,) * len(shape))
    rope = pl.BlockSpec((tm, RET_QK), lambda b, i: (i, 0))
    chunk_states = pl.BlockSpec((None, n_sub, RET_QK, RET_W), lambda b, i: (b, i, 0, 0))
    y = pl.pallas_call(
        functools.partial(_ret_out_kernel, rotate=rotate, n_sub=n_sub),
        grid=(B, T // tm),
        in_specs=[
            pl.BlockSpec((None, tm, 1024), lambda b, i: (b, i, 0)),
            rope, rope,
            small((1, RET_QK)), small((1, RET_QK)), small((1, RET_HEADS * c)), small((1, RET_HEADS * c)),
            chunk_states, chunk_states,
            small((RET_W, RET_W)),
        ],
        out_specs=pl.BlockSpec((None, tm, RET_W), lambda b, i: (b, i, 0)),
        out_shape=jax.ShapeDtypeStruct((B, T, RET_W), BF16),
        compiler_params=_params("parallel", "parallel"),
        name="retention_outputs",
    )(p, cos, sin, dfl, dbl, dfw, dbw, sf_all, sb_all, ones_bd)
    return y, sf_fin, sb_fin


def _pool_kernel(prev_ref, x_ref, next_ref, w_ref, scale_ref, o_ref, *, tp, t_total):
    i = pl.program_id(1)
    x = x_ref[...]
    prev = jnp.where(i > 0, prev_ref[...], 0.0)
    nxt = jnp.where(i < pl.num_programs(1) - 1, next_ref[...], 0.0)
    xp = jnp.concatenate([prev, x, nxt], axis=0)
    rows = tp + 2 * HALO
    back = lambda a, s: pltpu.roll(a, s, axis=0)
    ahead = lambda a, s: pltpu.roll(a, rows - s, axis=0)
    a2 = xp + back(xp, 1)
    a4 = a2 + back(a2, 2)
    a8 = a4 + back(a4, 4)
    a16 = a8 + back(a8, 8)
    lane = lax.broadcasted_iota(jnp.int32, (rows, POOL_W), 1)
    group = lane // POOL_GROUP
    win = jnp.where(group == 0, a2, jnp.where(group == 1, ahead(a4, 1), jnp.where(group == 2, ahead(a8, 3), ahead(a16, 7))))
    win = win[HALO:HALO + tp]

    lane = lax.broadcasted_iota(jnp.int32, (tp, POOL_W), 1)
    half = jnp.left_shift(1, lane // POOL_GROUP)
    t = lax.broadcasted_iota(jnp.int32, (tp, POOL_W), 0) + i * tp
    count = jnp.minimum(t + half, t_total) - jnp.maximum(t - half, 0)
    d = win / count.astype(F32) - x
    y = jnp.dot(d.astype(BF16), w_ref[...], preferred_element_type=F32) * scale_ref[...]
    o_ref[...] = y.astype(o_ref.dtype)


def _pool(p, w_bd, scale):
    B, T, _ = p.shape
    tp = min(T, 1024)
    col = P_POOL // POOL_W
    hb = tp // HALO
    return pl.pallas_call(
        functools.partial(_pool_kernel, tp=tp, t_total=T),
        grid=(B, T // tp),
        in_specs=[
            pl.BlockSpec((None, HALO, POOL_W), lambda b, i: (b, jnp.maximum(i * hb - 1, 0), col)),
            pl.BlockSpec((None, tp, POOL_W), lambda b, i: (b, i, col)),
            pl.BlockSpec((None, HALO, POOL_W), lambda b, i: (b, jnp.minimum((i + 1) * hb, T // HALO - 1), col)),
            pl.BlockSpec((POOL_W, POOL_W), lambda b, i: (0, 0)),
            pl.BlockSpec((1, POOL_W), lambda b, i: (0, 0)),
        ],
        out_specs=pl.BlockSpec((None, tp, POOL_W), lambda b, i: (b, i, 0)),
        out_shape=jax.ShapeDtypeStruct((B, T, POOL_W), BF16),
        compiler_params=_params("parallel", "parallel"),
        name="pooling",
    )(p, p, p, w_bd, scale.reshape(1, POOL_W))


def _mla_proj_kernel(cq_ref, kr_ref, ckv_ref, gq_ref, gkv_ref, wq_ref, wkv_ref, cos_ref, sin_ref,
                     q_ref, k_ref, v_ref, *, rotate):
    scale = (MLA_NOPE + MLA_ROPE) ** -0.5 * LOG2_E
    cq = _rms(cq_ref[...], MLA_Q_RANK) * gq_ref[...]
    q = jnp.dot(cq.astype(BF16), wq_ref[...], preferred_element_type=F32)
    ckv = _rms(ckv_ref[...], MLA_KV_RANK) * gkv_ref[...]
    kv = jnp.dot(ckv.astype(BF16), wkv_ref[...], preferred_element_type=F32)
    kr = kr_ref[:, 0:HEAD_PAD]
    if rotate:
        kr = kr * cos_ref[...] + kr_ref[:, HEAD_PAD:2 * HEAD_PAD] * sin_ref[...]
    ones_col = (lax.broadcasted_iota(jnp.int32, (1, HEAD_PAD), 1) == MLA_DV).astype(F32)
    left = (pl.program_id(1) % 2 == 0).astype(F32)
    for h in range(MLA_HEADS):
        base = 2 * HEAD_PAD * h
        qh = q[:, base:base + HEAD_PAD]
        if rotate:
            qh = qh * cos_ref[...] + q[:, base + HEAD_PAD:base + 2 * HEAD_PAD] * sin_ref[...]
        q_ref[h] = (qh * scale).astype(BF16)
        k_ref[h] = (kv[:, base:base + HEAD_PAD] + kr).astype(BF16)
        vh = kv[:, base + HEAD_PAD:base + 2 * HEAD_PAD] + ones_col
        v_ref[h] = jnp.concatenate([vh * left, vh * (1.0 - left)], axis=-1).astype(BF16)


def _mla_projection(p, gq, gkv, wq, wkv, cos, sin, rotate):
    B, T, _ = p.shape
    tm = V_ROWS
    heads = pl.BlockSpec((None, MLA_HEADS, tm, HEAD_PAD), lambda b, i: (b, 0, i, 0))
    shape = jax.ShapeDtypeStruct((B, MLA_HEADS, T, HEAD_PAD), BF16)
    v_heads = pl.BlockSpec((None, MLA_HEADS, tm, 2 * HEAD_PAD), lambda b, i: (b, 0, i, 0))
    v_shape = jax.ShapeDtypeStruct((B, MLA_HEADS, T, 2 * HEAD_PAD), BF16)
    const = lambda shape_: _resident(shape_, lambda b, i: (0, 0))
    return pl.pallas_call(
        functools.partial(_mla_proj_kernel, rotate=rotate),
        grid=(B, T // tm),
        in_specs=[
            pl.BlockSpec((None, tm, 512), lambda b, i: (b, i, P_CQ // 512)),
            pl.BlockSpec((None, tm, 256), lambda b, i: (b, i, P_KR // 256)),
            pl.BlockSpec((None, tm, 256), lambda b, i: (b, i, P_CKV // 256)),
            const((1, 512)), const((1, 256)), const((512, 2 * HEAD_PAD * MLA_HEADS)),
            const((256, 2 * HEAD_PAD * MLA_HEADS)),
            pl.BlockSpec((tm, HEAD_PAD), lambda b, i: (i, 0)),
            pl.BlockSpec((tm, HEAD_PAD), lambda b, i: (i, 0)),
        ],
        out_specs=[heads, heads, v_heads],
        out_shape=[shape, shape, v_shape],
        compiler_params=_params("parallel", "parallel"),
        name="mla_projection",
    )(p, p, p, gq, gkv, wq, wkv, cos, sin)


def _attn_kernel(q_ref, kc_ref, vc_ref, *rest, tk, n_lat, heads):
    if n_lat:
        kl_ref, vl_ref, o_ref, m_ref, acc_ref, s_ref = rest
    else:
        o_ref, m_ref, acc_ref = rest
    m_ref[...] = jnp.full_like(m_ref, -1e30)
    acc_ref[...] = jnp.zeros_like(acc_ref)

    def scores(h, k):
        return lax.dot_general(q_ref[h], k, (((1,), (1,)), ((), ())), preferred_element_type=F32)

    def update(h, s, v):
        m_old = m_ref[h]
        m_new = jnp.maximum(m_old[:, 0:1], jnp.max(s, axis=-1, keepdims=True))
        p = jnp.exp2(s - m_new)
        alpha = jnp.exp2(m_old - m_new)
        pv = jnp.dot(p.astype(BF16), v, preferred_element_type=F32)
        acc_ref[h] = jnp.concatenate([alpha, alpha], axis=-1) * acc_ref[h] + pv
        m_ref[h] = jnp.broadcast_to(m_new, m_old.shape)

    def lat(ref, h, j):
        return ref[h, pl.ds(pl.multiple_of(j * tk, tk), tk), :]

    def half(j, cur, nxt):
        for h in range(heads):
            if nxt is not None:
                s_ref[nxt, h] = scores(h, lat(kl_ref, h, j + 1))
            update(h, s_ref[cur, h], lat(vl_ref, h, j))

    ctx_scores = [scores(h, kc_ref[h]) for h in range(heads)]
    if n_lat:
        for h in range(heads):
            s_ref[0, h] = scores(h, lat(kl_ref, h, 0))
    for h in range(heads):
        update(h, ctx_scores[h], vc_ref[h])
    if n_lat:
        def body(jj, carry):
            half(2 * jj, 0, 1)
            half(2 * jj + 1, 1, 0)
            return carry

        lax.fori_loop(0, n_lat // 2 - 1, body, 0)
        half(n_lat - 2, 0, 1)
        half(n_lat - 1, 1, None)
    outs = []
    for h in range(heads):
        acc = acc_ref[h][:, 0:HEAD_PAD] + acc_ref[h][:, HEAD_PAD:2 * HEAD_PAD]
        outs.append(acc[:, 0:MLA_DV] / acc[:, MLA_DV:MLA_DV + 1])
    o_ref[...] = jnp.concatenate(outs, axis=-1).astype(o_ref.dtype)


def _attention(q, k_ctx, v_ctx, k_lat=None, v_lat=None):
    B, H, Tq, _ = q.shape
    Tc = k_ctx.shape[2]
    heads = 2
    tq = min(Tq, 1024)
    tk = 512
    n_lat = 0 if k_lat is None else k_lat.shape[2] // tk
    assert tk % (2 * V_ROWS) == 0 and Tc % V_ROWS == 0
    whole = lambda t, w: pl.BlockSpec((None, heads, t, w), lambda b, g, i: (b, g, 0, 0))
    in_specs = [pl.BlockSpec((None, heads, tq, HEAD_PAD), lambda b, g, i: (b, g, i, 0)),
                whole(Tc, HEAD_PAD), whole(Tc, 2 * HEAD_PAD)]
    args = [q, k_ctx, v_ctx]
    scratch = [pltpu.VMEM((heads, tq, HEAD_PAD), F32), pltpu.VMEM((heads, tq, 2 * HEAD_PAD), F32)]
    if n_lat:
        assert n_lat % 2 == 0
        in_specs += [whole(k_lat.shape[2], HEAD_PAD), whole(k_lat.shape[2], 2 * HEAD_PAD)]
        args += [k_lat, v_lat]
        scratch.append(pltpu.VMEM((2, heads, tq, tk), F32))
    return pl.pallas_call(
        functools.partial(_attn_kernel, tk=tk, n_lat=n_lat, heads=heads),
        grid=(B, H // heads, Tq // tq),
        in_specs=in_specs,
        out_specs=pl.BlockSpec((None, tq, heads * MLA_DV), lambda b, g, i: (b, i, g)),
        out_shape=jax.ShapeDtypeStruct((B, Tq, H * MLA_DV), BF16),
        scratch_shapes=scratch,
        compiler_params=_params("parallel", "parallel", "parallel"),
        name="attention",
    )(*args)


def _outproj_kernel(x_ref, yr_ref, ym_ref, yp_ref, gate_ref, wr_ref, wm_ref, wp_ref, o_ref):
    y = jnp.dot(yr_ref[...], wr_ref[...], preferred_element_type=F32)
    y = y + jnp.dot(ym_ref[...], wm_ref[...], preferred_element_type=F32)
    y = y + jnp.dot(yp_ref[...], wp_ref[...], preferred_element_type=F32)
    o_ref[...] = x_ref[...] + gate_ref[...] * y


def _out_projection(x, y_ret, y_mla, y_pool, mod, row_of_batch, w_r, w_m, w_p):
    B, T, D = x.shape
    tm = min(T, 512)
    rows = lambda w: pl.BlockSpec((None, tm, w), lambda b, i: (b, i, 0))
    const = lambda shape: _resident(shape, lambda b, i: (0, 0))
    return pl.pallas_call(
        _outproj_kernel,
        grid=(B, T // tm),
        in_specs=[
            rows(D), rows(RET_W), rows(MLA_W), rows(POOL_W),
            pl.BlockSpec((None, 1, D), _mod_index(2, row_of_batch)),
            const((RET_W, D)), const((MLA_W, D)), const((POOL_W, D)),
        ],
        out_specs=rows(D),
        out_shape=jax.ShapeDtypeStruct((B, T, D), F32),
        compiler_params=_params("parallel", "parallel"),
        name="out_projection",
    )(x, y_ret, y_mla, y_pool, mod, w_r, w_m, w_p)


def _ffn_kernel(xp_ref, x_ref, xn_ref, g_ref, sh_ref, sc_ref, gate_ref, wa_ref, wb_ref, cwa_ref, cwb_ref,
                cba_ref, cbb_ref, wd_ref, fg_ref, o_ref, h_ref, ua_ref, ub_ref, act_ref, *, tm, n_chunks, final):
    i = pl.program_id(1)
    d = x_ref.shape[-1]

    def norm_mod(x):
        h = _rms(x, d) * g_ref[...]
        return h * (1.0 + sc_ref[...]) + sh_ref[...]

    h_ref[0:HALO, :] = jnp.where(i > 0, norm_mod(xp_ref[...]), 0.0).astype(BF16)
    h_ref[HALO:HALO + tm, :] = norm_mod(x_ref[...]).astype(BF16)
    h_ref[HALO + tm:2 * HALO + tm, :] = jnp.where(i < pl.num_programs(1) - 1, norm_mod(xn_ref[...]), 0.0).astype(BF16)
    rows = tm + 2 * HALO

    def up(w_ref, u_ref, f, slot):
        u_ref[slot] = jnp.dot(h_ref[...], w_ref[f], preferred_element_type=F32)

    def conv(u_ref, slot, cw_ref, cb_ref, f):
        cw = cw_ref[f]
        u = u_ref[slot]
        y = pltpu.roll(u, 1, axis=0) * cw[0:1] + u * cw[1:2] + pltpu.roll(u, rows - 1, axis=0) * cw[2:3]
        return y[HALO:HALO + tm] + cb_ref[f]

    def half(f, cur, nxt):
        if nxt is not None:
            up(wa_ref, ua_ref, f + 1, nxt)
        a = _silu(conv(ua_ref, cur, cwa_ref, cba_ref, f))
        if nxt is not None:
            up(wb_ref, ub_ref, f + 1, nxt)
        b = conv(ub_ref, cur, cwb_ref, cbb_ref, f)
        act_ref[f] = (a * b).astype(BF16)

    def pair(jj, carry):
        half(2 * jj, 0, 1)
        half(2 * jj + 1, 1, 0)
        return carry

    up(wa_ref, ua_ref, 0, 0)
    up(wb_ref, ub_ref, 0, 0)
    lax.fori_loop(0, (n_chunks - 1) // 2, pair, 0)
    half(n_chunks - 1, 0, None)
    acc = jnp.dot(act_ref[0], wd_ref[0], preferred_element_type=F32)
    for f in range(1, n_chunks):
        acc = acc + jnp.dot(act_ref[f], wd_ref[f], preferred_element_type=F32)
    y = x_ref[...] + gate_ref[...] * acc
    if final:
        y = _rms(y, d) * fg_ref[...]
    o_ref[...] = y


def _ffn(x, g, mod, row_of_batch, wa, wb, cwa, cwb, cba, cbb, wd, final_g, final):
    B, T, D = x.shape
    n_chunks, _, fc = wa.shape
    assert n_chunks % 2 == 1
    tm = min(T, 512)
    hb = tm // HALO
    vec = lambda sec: pl.BlockSpec((None, 1, D), _mod_index(sec, row_of_batch))
    const3 = lambda a: _resident(a.shape, lambda b, i: (0, 0, 0))
    return pl.pallas_call(
        functools.partial(_ffn_kernel, tm=tm, n_chunks=n_chunks, final=final),
        grid=(B, T // tm),
        in_specs=[
            pl.BlockSpec((None, HALO, D), lambda b, i: (b, jnp.maximum(i * hb - 1, 0), 0)),
            pl.BlockSpec((None, tm, D), lambda b, i: (b, i, 0)),
            pl.BlockSpec((None, HALO, D), lambda b, i: (b, jnp.minimum((i + 1) * hb, T // HALO - 1), 0)),
            pl.BlockSpec((1, D), lambda b, i: (0, 0)),
            vec(3), vec(4), vec(5),
            const3(wa), const3(wb), const3(cwa), const3(cwb), const3(cba), const3(cbb), const3(wd),
            pl.BlockSpec((1, D), lambda b, i: (0, 0)),
        ],
        out_specs=pl.BlockSpec((None, tm, D), lambda b, i: (b, i, 0)),
        out_shape=jax.ShapeDtypeStruct((B, T, D), F32),
        scratch_shapes=[pltpu.VMEM((tm + 2 * HALO, D), BF16), pltpu.VMEM((2, tm + 2 * HALO, fc), F32),
                        pltpu.VMEM((2, tm + 2 * HALO, fc), F32), pltpu.VMEM((n_chunks, tm, fc), BF16)],
        compiler_params=_params("parallel", "parallel"),
        name="conv_ffn",
    )(x, x, x, g.reshape(1, D), mod, mod, mod, wa, wb, cwa, cwb, cba, cbb, wd, final_g.reshape(1, D))


def _swap_halves(w, groups, width):
    lead = w.shape[:-1]
    return jnp.flip(w.reshape(*lead, groups, 2, width // 2), axis=-2).reshape(*lead, groups * width)


def _layout_in_proj(w_in):
    D = w_in.shape[0]
    offs = [0]
    for s in (RET_QK, RET_QK, RET_W, RET_W, MLA_Q_RANK, MLA_KV_RANK, MLA_ROPE, POOL_W):
        offs.append(offs[-1] + s)
    rq, rk, rv, rg, cq, ckv, kr, pool = [w_in[:, offs[j]:offs[j + 1]] for j in range(8)]
    z = lambda n: jnp.zeros((D, n), w_in.dtype)
    place = lambda a: jnp.concatenate([z(MLA_NOPE), a, z(HEAD_PAD - MLA_NOPE - MLA_ROPE)], axis=1)
    cols = [rk, _swap_halves(rk, RET_HEADS, RET_DK), rv, rq, _swap_halves(rq, RET_HEADS, RET_DK), rg,
            cq, z(512 - MLA_Q_RANK), place(kr), place(_swap_halves(kr, 1, MLA_ROPE)), ckv, pool]
    return jnp.concatenate(cols, axis=1).astype(BF16)


def _layout_uq(w_uq):
    r = w_uq.shape[0]
    w = w_uq.reshape(r, MLA_HEADS, MLA_NOPE + MLA_ROPE)
    nope, rope = w[..., :MLA_NOPE], w[..., MLA_NOPE:]
    z = lambda n: jnp.zeros((r, MLA_HEADS, n), w_uq.dtype)
    tail = HEAD_PAD - MLA_NOPE - MLA_ROPE
    per_head = jnp.concatenate([nope, rope, z(tail), z(MLA_NOPE), _swap_halves(rope, 1, MLA_ROPE), z(tail)], axis=-1)
    w = per_head.reshape(r, MLA_HEADS * 2 * HEAD_PAD)
    return jnp.concatenate([w, jnp.zeros((512 - r, w.shape[1]), w.dtype)], axis=0).astype(BF16)


def _layout_ukv(w_ukv):
    r = w_ukv.shape[0]
    w = w_ukv.reshape(r, MLA_HEADS, MLA_NOPE + MLA_DV)
    z = lambda n: jnp.zeros((r, MLA_HEADS, n), w_ukv.dtype)
    per_head = jnp.concatenate([w[..., :MLA_NOPE], z(HEAD_PAD - MLA_NOPE), w[..., MLA_NOPE:], z(HEAD_PAD - MLA_DV)], axis=-1)
    return per_head.reshape(r, MLA_HEADS * 2 * HEAD_PAD).astype(BF16)


def _chunk_cols(w, fc):
    r, f = w.shape
    return w.reshape(r, f // fc, fc).transpose(1, 0, 2)


def _rope_tables(t):
    rows = t // GRID_W
    row = jnp.repeat(jnp.arange(rows, dtype=F32), GRID_W)
    col = jnp.tile(jnp.arange(GRID_W, dtype=F32), rows)
    n_freq = MLA_ROPE // 4
    inv = ROPE_THETA ** (-jnp.arange(n_freq, dtype=F32) / n_freq)
    ang = jnp.concatenate([row[:, None] * inv, col[:, None] * inv], axis=-1)
    cos, sin = jnp.cos(ang), jnp.sin(ang)
    cc = jnp.concatenate([cos, cos], axis=-1)
    ss = jnp.concatenate([-sin, sin], axis=-1)
    ret_cos, ret_sin = jnp.tile(cc, (1, RET_HEADS)), jnp.tile(ss, (1, RET_HEADS))
    tail = HEAD_PAD - MLA_NOPE - MLA_ROPE
    one, zero = jnp.ones((t, MLA_NOPE), F32), jnp.zeros((t, MLA_NOPE), F32)
    mla_cos = jnp.concatenate([one, cc, jnp.ones((t, tail), F32)], axis=-1)
    mla_sin = jnp.concatenate([zero, ss, jnp.zeros((t, tail), F32)], axis=-1)
    return ret_cos, ret_sin, mla_cos, mla_sin


def kernel(x, c, ctx, c_ctx, w_mod, b_mod, norm1_g, w_in, ret_decay_f, ret_decay_b, mla_q_norm_g, w_uq,
           mla_kv_norm_g, w_ukv, pool_w, pool_scale, w_out, norm2_g, w_up, conv_w, conv_b, w_down,
           final_norm_g):
    B, T, D = x.shape
    L = w_mod.shape[0]
    d_ff = w_down.shape[1]
    fc = 256
    assert T % 512 == 0 and ctx.shape[1] % RET_CHUNK == 0 and d_ff % fc == 0 and B + 1 <= SUBLANES

    ret_cos, ret_sin, mla_cos, mla_sin = _rope_tables(T)
    c_rows = jnp.concatenate([c, c_ctx[None, :], jnp.zeros((SUBLANES - B - 1, D), c.dtype)], axis=0)
    mod_all = _modulation(c_rows, w_mod.astype(BF16), b_mod)
    lat_row = lambda b: b
    ctx_row = lambda b: B

    xc = ctx
    zero_state = jnp.zeros((B, RET_QK, RET_W), F32)
    for l in range(L):
        with_ctx_out = l < L - 1
        mod = mod_all[l].reshape(SUBLANES * 6, 1, D)
        w_p = _layout_in_proj(w_in[l])
        wq = _layout_uq(w_uq[l])
        wkv = _layout_ukv(w_ukv[l])
        gq = jnp.concatenate([mla_q_norm_g[l], jnp.zeros((512 - MLA_Q_RANK,), F32)]).reshape(1, 512)
        gkv = mla_kv_norm_g[l].reshape(1, MLA_KV_RANK)
        pool_bd = jax.scipy.linalg.block_diag(*[pool_w[l, g] for g in range(len(POOL_WINDOWS))]).astype(BF16)
        wo = w_out[l].astype(BF16)
        wo_r, wo_m, wo_p = wo[:RET_W], wo[RET_W:RET_W + MLA_W], wo[RET_W + MLA_W:]
        wa, wb = _chunk_cols(w_up[l, :, :d_ff].astype(BF16), fc), _chunk_cols(w_up[l, :, d_ff:].astype(BF16), fc)
        cwa, cwb = _chunk_cols(conv_w[l, :, :d_ff], fc), _chunk_cols(conv_w[l, :, d_ff:], fc)
        cba, cbb = _chunk_cols(conv_b[l, None, :d_ff], fc), _chunk_cols(conv_b[l, None, d_ff:], fc)
        wd = w_down[l].astype(BF16).reshape(d_ff // fc, fc, D)

        p_lat = _in_projection(x, norm1_g[l], mod, lat_row, w_p)
        p_ctx = _in_projection(xc, norm1_g[l], mod, ctx_row, w_p)

        yr_ctx, s_f, s_b = _retention(p_ctx, ret_cos, ret_sin, ret_decay_f[l], ret_decay_b[l], zero_state, zero_state, False)
        yr_lat, _, _ = _retention(p_lat, ret_cos, ret_sin, ret_decay_f[l], ret_decay_b[l], s_f, s_b, True)

        q_ctx, k_ctx, v_ctx = _mla_projection(p_ctx, gq, gkv, wq, wkv, mla_cos, mla_sin, False)
        q_lat, k_lat, v_lat = _mla_projection(p_lat, gq, gkv, wq, wkv, mla_cos, mla_sin, True)
        ym_lat = _attention(q_lat, k_ctx, v_ctx, k_lat, v_lat)

        yp_lat = _pool(p_lat, pool_bd, pool_scale[l])
        x = _out_projection(x, yr_lat, ym_lat, yp_lat, mod, lat_row, wo_r, wo_m, wo_p)
        x = _ffn(x, norm2_g[l], mod, lat_row, wa, wb, cwa, cwb, cba, cbb, wd, final_norm_g, not with_ctx_out)
        if with_ctx_out:
            ym_ctx = _attention(q_ctx, k_ctx, v_ctx)
            yp_ctx = _pool(p_ctx, pool_bd, pool_scale[l])
            xc = _out_projection(xc, yr_ctx, ym_ctx, yp_ctx, mod, ctx_row, wo_r, wo_m, wo_p)
            xc = _ffn(xc, norm2_g[l], mod, ctx_row, wa, wb, cwa, cwb, cba, cbb, wd, final_norm_g, False)
    return x
```

```python
import functools

import jax
import jax.numpy as jnp
from jax import lax
from jax.experimental import pallas as pl
from jax.experimental.pallas import tpu as pltpu

BF16 = jnp.bfloat16
F32 = jnp.float32

GRID_W = 64
ROPE_THETA = 10000.0
NORM_EPS = 1e-6
RET_HEADS = 4
RET_DK = 32
RET_DV = 64
MLA_HEADS = 8
MLA_Q_RANK = 384
MLA_KV_RANK = 256
MLA_NOPE = 64
MLA_ROPE = 32
MLA_DV = 64
POOL_WINDOWS = (2, 4, 8, 16)
POOL_GROUP = 64

LANES = 128
SUBLANES = 8
BF16_ROWS = 16
VMEM_LIMIT_BYTES = 56 * 1024 * 1024

RET_W = RET_HEADS * RET_DV
RET_QK = RET_HEADS * RET_DK
MLA_W = MLA_HEADS * MLA_DV
POOL_W = len(POOL_WINDOWS) * POOL_GROUP
HEAD_PAD = 128
HALO = 16
LOG2_E = 1.4426950408889634
V_ROWS = 256
RET_CHUNK = 128

P_RET = 0
P_CQ = 1024
P_KR = 1536
P_CKV = 1792
P_POOL = 2048
P_WIDTH = 2304


def _params(*semantics):
    return pltpu.CompilerParams(dimension_semantics=semantics, vmem_limit_bytes=VMEM_LIMIT_BYTES)


def _resident(block_shape, index_map):
    return pl.BlockSpec(block_shape, index_map, pipeline_mode=pl.Buffered(1))


def _rms(x, width):
    return x * lax.rsqrt(jnp.sum(x * x, axis=-1, keepdims=True) * (1.0 / width) + NORM_EPS)


def _silu(x):
    return x * jax.nn.sigmoid(x)


def _log_sigmoid(d):
    return jnp.minimum(d, 0.0) - jnp.log1p(jnp.exp(-jnp.abs(d)))


def _mod_kernel(c_ref, w_ref, b_ref, o_ref):
    s = _silu(c_ref[...])
    o_ref[...] = jnp.dot(s.astype(BF16), w_ref[...].astype(BF16), preferred_element_type=F32) + b_ref[...]


def _modulation(c_rows, w_mod, b_mod):
    L, D, W = w_mod.shape
    tn = 1536
    return pl.pallas_call(
        _mod_kernel,
        grid=(L, W // tn),
        in_specs=[
            pl.BlockSpec((SUBLANES, D), lambda l, j: (0, 0)),
            pl.BlockSpec((None, D, tn), lambda l, j: (l, 0, j)),
            pl.BlockSpec((None, 1, tn), lambda l, j: (l, 0, j)),
        ],
        out_specs=pl.BlockSpec((None, SUBLANES, tn), lambda l, j: (l, 0, j)),
        out_shape=jax.ShapeDtypeStruct((L, SUBLANES, W), F32),
        compiler_params=_params("parallel", "parallel"),
        name="modulation",
    )(c_rows, w_mod, b_mod.reshape(L, 1, W))


def _inproj_kernel(x_ref, g_ref, sh_ref, sc_ref, w_ref, o_ref):
    x = x_ref[...]
    h = _rms(x, x.shape[-1]) * g_ref[...]
    h = h * (1.0 + sc_ref[...]) + sh_ref[...]
    o_ref[...] = jnp.dot(h.astype(BF16), w_ref[...], preferred_element_type=F32)


def _mod_index(section, row_of_batch):
    def index(b, *_):
        return (row_of_batch(b) * 6 + section, 0, 0)

    return index


def _in_projection(x, g, mod, row_of_batch, w):
    B, T, D = x.shape
    tm = min(T, 512)
    vec = lambda sec: pl.BlockSpec((None, 1, D), _mod_index(sec, row_of_batch))
    return pl.pallas_call(
        _inproj_kernel,
        grid=(B, T // tm),
        in_specs=[
            pl.BlockSpec((None, tm, D), lambda b, i: (b, i, 0)),
            pl.BlockSpec((1, D), lambda b, i: (0, 0)),
            vec(0),
            vec(1),
            _resident((D, P_WIDTH), lambda b, i: (0, 0)),
        ],
        out_specs=pl.BlockSpec((None, tm, P_WIDTH), lambda b, i: (b, i, 0)),
        out_shape=jax.ShapeDtypeStruct((B, T, P_WIDTH), F32),
        compiler_params=_params("parallel", "parallel"),
        name="in_projection",
    )(x, g.reshape(1, D), mod, mod, w)


def _ret_rope(a, a_swap, cos, sin, rotate):
    if not rotate:
        return a
    return a * cos + a_swap * sin


def _ret_state_kernel(pf_ref, pb_ref, cf_ref, sf_ref, cb_ref, sb_ref, dfl_ref, dbl_ref, dfc_ref, dbc_ref,
                      s0f_ref, s0b_ref, sf_all_ref, sb_all_ref, sf_fin_ref, sb_fin_ref, stf_ref, stb_ref,
                      *, rotate):
    c = RET_CHUNK
    i = pl.program_id(0)

    @pl.when(i == 0)
    def _():
        stf_ref[...] = s0f_ref[...]
        stb_ref[...] = s0b_ref[...]

    sf_all_ref[...] = stf_ref[...]
    sb_all_ref[...] = stb_ref[...]

    n = lax.broadcasted_iota(jnp.int32, (c, RET_QK), 0).astype(F32)
    rows = lax.broadcasted_iota(jnp.int32, (RET_QK, RET_W), 0)
    cols = lax.broadcasted_iota(jnp.int32, (RET_QK, RET_W), 1)
    head_block = (rows // RET_DK) == (cols // RET_DV)

    def update(p_ref, cos_ref, sin_ref, lg_lane, lg_col, pos_pow, st_ref):
        k_decay = jnp.exp(lg_lane * pos_pow) * (RET_DK ** -0.5)
        carry_decay = jnp.exp(lg_col * float(c))
        for b in range(p_ref.shape[0]):
            k = _ret_rope(p_ref[b, :, 0:RET_QK], p_ref[b, :, RET_QK:2 * RET_QK], cos_ref[...], sin_ref[...], rotate)
            v = p_ref[b, :, 2 * RET_QK:2 * RET_QK + RET_W]
            kv = lax.dot_general((k * k_decay).astype(BF16), v.astype(BF16), (((0,), (0,)), ((), ())),
                                 preferred_element_type=F32)
            st_ref[b] = st_ref[b] * carry_decay + jnp.where(head_block, kv, 0.0)

    update(pf_ref, cf_ref, sf_ref, _log_sigmoid(dfl_ref[...]), _log_sigmoid(dfc_ref[...]), (c - 1.0) - n, stf_ref)
    update(pb_ref, cb_ref, sb_ref, _log_sigmoid(dbl_ref[...]), _log_sigmoid(dbc_ref[...]), n, stb_ref)

    @pl.when(i == pl.num_programs(0) - 1)
    def _():
        sf_fin_ref[...] = stf_ref[...]
        sb_fin_ref[...] = stb_ref[...]


def _ret_out_kernel(p_ref, cos_ref, sin_ref, dfl_ref, dbl_ref, dfw_ref, dbw_ref, sf_ref, sb_ref, ones_ref,
                    o_ref, *, rotate, n_sub):
    c = RET_CHUNK
    kr = lax.broadcasted_iota(jnp.int32, (RET_HEADS * c, RET_QK), 0) // c
    kl = lax.broadcasted_iota(jnp.int32, (RET_HEADS * c, RET_QK), 1) // RET_DK
    vr = lax.broadcasted_iota(jnp.int32, (RET_HEADS * c, RET_W), 0) // c
    vl = lax.broadcasted_iota(jnp.int32, (RET_HEADS * c, RET_W), 1) // RET_DV

    lgf_w = _log_sigmoid(dfw_ref[...])
    lgb_w = _log_sigmoid(dbw_ref[...])
    nq = lax.broadcasted_iota(jnp.int32, (c, RET_HEADS * c), 0)
    mk = lax.broadcasted_iota(jnp.int32, (c, RET_HEADS * c), 1) % c
    diff = (nq - mk).astype(F32)
    dmat = jnp.exp(jnp.where(diff >= 0.0, lgf_w, -lgb_w) * diff)
    n = lax.broadcasted_iota(jnp.int32, (c, RET_QK), 0).astype(F32)
    qf_decay = jnp.exp(_log_sigmoid(dfl_ref[...]) * (n + 1.0))
    qb_decay = jnp.exp(_log_sigmoid(dbl_ref[...]) * (float(c) - n))

    for s in range(n_sub):
        r = slice(s * c, (s + 1) * c)
        cos, sin = cos_ref[r, :], sin_ref[r, :]
        k = _ret_rope(p_ref[r, 0:RET_QK], p_ref[r, RET_QK:2 * RET_QK], cos, sin, rotate) * (RET_DK ** -0.5)
        v = p_ref[r, 2 * RET_QK:2 * RET_QK + RET_W]
        q = _ret_rope(p_ref[r, 512:512 + RET_QK], p_ref[r, 512 + RET_QK:512 + 2 * RET_QK], cos, sin, rotate)
        gate = p_ref[r, 768:768 + RET_W]
        k_stack = jnp.where(kr == kl, jnp.tile(k, (RET_HEADS, 1)), 0.0).astype(BF16)
        v_stack = jnp.where(vr == vl, jnp.tile(v, (RET_HEADS, 1)), 0.0).astype(BF16)

        scores = lax.dot_general(q.astype(BF16), k_stack, (((1,), (1,)), ((), ())), preferred_element_type=F32)
        a = (scores * dmat).astype(BF16)
        o = jnp.dot(a, v_stack, preferred_element_type=F32)
        o = o + jnp.dot((q * qf_decay).astype(BF16), sf_ref[s].astype(BF16), preferred_element_type=F32)
        o = o + jnp.dot((q * qb_decay).astype(BF16), sb_ref[s].astype(BF16), preferred_element_type=F32)

        sq = o * o
        hi = sq.astype(BF16)
        lo = (sq - hi.astype(F32)).astype(BF16)
        ssum = (jnp.dot(hi, ones_ref[...], preferred_element_type=F32)
                + jnp.dot(lo, ones_ref[...], preferred_element_type=F32))
        o = o * lax.rsqrt(ssum * (1.0 / RET_DV) + NORM_EPS)
        o_ref[r, :] = (o * _silu(gate)).astype(o_ref.dtype)


def _retention(p, cos, sin, decay_f, decay_b, s0f, s0b, rotate):
    B, T, _ = p.shape
    c = RET_CHUNK
    nc = T // c
    rep = lambda d, r: jnp.repeat(d.astype(F32), r)
    dfl, dbl = rep(decay_f, RET_DK).reshape(1, RET_QK), rep(decay_b, RET_DK).reshape(1, RET_QK)
    dfc, dbc = dfl.reshape(RET_QK, 1), dbl.reshape(RET_QK, 1)
    dfw, dbw = rep(decay_f, c).reshape(1, RET_HEADS * c), rep(decay_b, c).reshape(1, RET_HEADS * c)
    small1 = lambda shape: pl.BlockSpec(shape, lambda i: (0,) * len(shape))
    rope_f = pl.BlockSpec((c, RET_QK), lambda i: (i, 0))
    rope_b = pl.BlockSpec((c, RET_QK), lambda i: (nc - 1 - i, 0))
    state = pl.BlockSpec((B, RET_QK, RET_W), lambda i: (0, 0, 0))

    sf_all, sb_all, sf_fin, sb_fin = pl.pallas_call(
        functools.partial(_ret_state_kernel, rotate=rotate),
        grid=(nc,),
        in_specs=[
            pl.BlockSpec((B, c, 512), lambda i: (0, i, 0)),
            pl.BlockSpec((B, c, 512), lambda i: (0, nc - 1 - i, 0)),
            rope_f, rope_f, rope_b, rope_b,
            small1((1, RET_QK)), small1((1, RET_QK)), small1((RET_QK, 1)), small1((RET_QK, 1)),
            state, state,
        ],
        out_specs=[
            pl.BlockSpec((B, None, RET_QK, RET_W), lambda i: (0, i, 0, 0)),
            pl.BlockSpec((B, None, RET_QK, RET_W), lambda i: (0, nc - 1 - i, 0, 0)),
            state, state,
        ],
        out_shape=[
            jax.ShapeDtypeStruct((B, nc, RET_QK, RET_W), F32),
            jax.ShapeDtypeStruct((B, nc, RET_QK, RET_W), F32),
            jax.ShapeDtypeStruct((B, RET_QK, RET_W), F32),
            jax.ShapeDtypeStruct((B, RET_QK, RET_W), F32),
        ],
        scratch_shapes=[pltpu.VMEM((B, RET_QK, RET_W), F32), pltpu.VMEM((B, RET_QK, RET_W), F32)],
        compiler_params=_params("arbitrary"),
        name="retention_states",
    )(p, p, cos, sin, cos, sin, dfl, dbl, dfc, dbc, s0f, s0b)

    gi = lax.broadcasted_iota(jnp.int32, (RET_W, RET_W), 0) // RET_DV
    gj = lax.broadcasted_iota(jnp.int32, (RET_W, RET_W), 1) // RET_DV
    ones_bd = (gi == gj).astype(BF16)
    n_sub = min(nc, 4)
    tm = n_sub * c
    small = lambda shape: pl.BlockSpec(shape, lambda b, i: (0,) * len(shape))
    rope = pl.BlockSpec((tm, RET_QK), lambda b, i: (i, 0))
    chunk_states = pl.BlockSpec((None, n_sub, RET_QK, RET_W), lambda b, i: (b, i, 0, 0))
    y = pl.pallas_call(
        functools.partial(_ret_out_kernel, rotate=rotate, n_sub=n_sub),
        grid=(B, T // tm),
        in_specs=[
            pl.BlockSpec((None, tm, 1024), lambda b, i: (b, i, 0)),
            rope, rope,
            small((1, RET_QK)), small((1, RET_QK)), small((1, RET_HEADS * c)), small((1, RET_HEADS * c)),
            chunk_states, chunk_states,
            small((RET_W, RET_W)),
        ],
        out_specs=pl.BlockSpec((None, tm, RET_W), lambda b, i: (b, i, 0)),
        out_shape=jax.ShapeDtypeStruct((B, T, RET_W), BF16),
        compiler_params=_params("parallel", "parallel"),
        name="retention_outputs",
    )(p, cos, sin, dfl, dbl, dfw, dbw, sf_all, sb_all, ones_bd)
    return y, sf_fin, sb_fin


def _pool_kernel(prev_ref, x_ref, next_ref, w_ref, scale_ref, o_ref, *, tp, t_total):
    i = pl.program_id(1)
    x = x_ref[...]
    prev = jnp.where(i > 0, prev_ref[...], 0.0)
    nxt = jnp.where(i < pl.num_programs(1) - 1, next_ref[...], 0.0)
    xp = jnp.concatenate([prev, x, nxt], axis=0)
    rows = tp + 2 * HALO
    back = lambda a, s: pltpu.roll(a, s, axis=0)
    ahead = lambda a, s: pltpu.roll(a, rows - s, axis=0)
    a2 = xp + back(xp, 1)
    a4 = a2 + back(a2, 2)
    a8 = a4 + back(a4, 4)
    a16 = a8 + back(a8, 8)
    lane = lax.broadcasted_iota(jnp.int32, (rows, POOL_W), 1)
    group = lane // POOL_GROUP
    win = jnp.where(group == 0, a2, jnp.where(group == 1, ahead(a4, 1), jnp.where(group == 2, ahead(a8, 3), ahead(a16, 7))))
    win = win[HALO:HALO + tp]

    lane = lax.broadcasted_iota(jnp.int32, (tp, POOL_W), 1)
    half = jnp.left_shift(1, lane // POOL_GROUP)
    t = lax.broadcasted_iota(jnp.int32, (tp, POOL_W), 0) + i * tp
    count = jnp.minimum(t + half, t_total) - jnp.maximum(t - half, 0)
    d = win / count.astype(F32) - x
    y = jnp.dot(d.astype(BF16), w_ref[...], preferred_element_type=F32) * scale_ref[...]
    o_ref[...] = y.astype(o_ref.dtype)


def _pool(p, w_bd, scale):
    B, T, _ = p.shape
    tp = min(T, 1024)
    col = P_POOL // POOL_W
    hb = tp // HALO
    return pl.pallas_call(
        functools.partial(_pool_kernel, tp=tp, t_total=T),
        grid=(B, T // tp),
        in_specs=[
            pl.BlockSpec((None, HALO, POOL_W), lambda b, i: (b, jnp.maximum(i * hb - 1, 0), col)),
            pl.BlockSpec((None, tp, POOL_W), lambda b, i: (b, i, col)),
            pl.BlockSpec((None, HALO, POOL_W), lambda b, i: (b, jnp.minimum((i + 1) * hb, T // HALO - 1), col)),
            pl.BlockSpec((POOL_W, POOL_W), lambda b, i: (0, 0)),
            pl.BlockSpec((1, POOL_W), lambda b, i: (0, 0)),
        ],
        out_specs=pl.BlockSpec((None, tp, POOL_W), lambda b, i: (b, i, 0)),
        out_shape=jax.ShapeDtypeStruct((B, T, POOL_W), BF16),
        compiler_params=_params("parallel", "parallel"),
        name="pooling",
    )(p, p, p, w_bd, scale.reshape(1, POOL_W))


def _mla_proj_kernel(cq_ref, kr_ref, ckv_ref, gq_ref, gkv_ref, wq_ref, wkv_ref, cos_ref, sin_ref,
                     q_ref, k_ref, v_ref, *, rotate):
    scale = (MLA_NOPE + MLA_ROPE) ** -0.5 * LOG2_E
    cq = _rms(cq_ref[...], MLA_Q_RANK) * gq_ref[...]
    q = jnp.dot(cq.astype(BF16), wq_ref[...], preferred_element_type=F32)
    ckv = _rms(ckv_ref[...], MLA_KV_RANK) * gkv_ref[...]
    kv = jnp.dot(ckv.astype(BF16), wkv_ref[...], preferred_element_type=F32)
    kr = kr_ref[:, 0:HEAD_PAD]
    if rotate:
        kr = kr * cos_ref[...] + kr_ref[:, HEAD_PAD:2 * HEAD_PAD] * sin_ref[...]
    ones_col = (lax.broadcasted_iota(jnp.int32, (1, HEAD_PAD), 1) == MLA_DV).astype(F32)
    left = (pl.program_id(1) % 2 == 0).astype(F32)
    for h in range(MLA_HEADS):
        base = 2 * HEAD_PAD * h
        qh = q[:, base:base + HEAD_PAD]
        if rotate:
            qh = qh * cos_ref[...] + q[:, base + HEAD_PAD:base + 2 * HEAD_PAD] * sin_ref[...]
        q_ref[h] = (qh * scale).astype(BF16)
        k_ref[h] = (kv[:, base:base + HEAD_PAD] + kr).astype(BF16)
        vh = kv[:, base + HEAD_PAD:base + 2 * HEAD_PAD] + ones_col
        v_ref[h] = jnp.concatenate([vh * left, vh * (1.0 - left)], axis=-1).astype(BF16)


def _mla_projection(p, gq, gkv, wq, wkv, cos, sin, rotate):
    B, T, _ = p.shape
    tm = V_ROWS
    heads = pl.BlockSpec((None, MLA_HEADS, tm, HEAD_PAD), lambda b, i: (b, 0, i, 0))
    shape = jax.ShapeDtypeStruct((B, MLA_HEADS, T, HEAD_PAD), BF16)
    v_heads = pl.BlockSpec((None, MLA_HEADS, tm, 2 * HEAD_PAD), lambda b, i: (b, 0, i, 0))
    v_shape = jax.ShapeDtypeStruct((B, MLA_HEADS, T, 2 * HEAD_PAD), BF16)
    const = lambda shape_: _resident(shape_, lambda b, i: (0, 0))
    return pl.pallas_call(
        functools.partial(_mla_proj_kernel, rotate=rotate),
        grid=(B, T // tm),
        in_specs=[
            pl.BlockSpec((None, tm, 512), lambda b, i: (b, i, P_CQ // 512)),
            pl.BlockSpec((None, tm, 256), lambda b, i: (b, i, P_KR // 256)),
            pl.BlockSpec((None, tm, 256), lambda b, i: (b, i, P_CKV // 256)),
            const((1, 512)), const((1, 256)), const((512, 2 * HEAD_PAD * MLA_HEADS)),
            const((256, 2 * HEAD_PAD * MLA_HEADS)),
            pl.BlockSpec((tm, HEAD_PAD), lambda b, i: (i, 0)),
            pl.BlockSpec((tm, HEAD_PAD), lambda b, i: (i, 0)),
        ],
        out_specs=[heads, heads, v_heads],
        out_shape=[shape, shape, v_shape],
        compiler_params=_params("parallel", "parallel"),
        name="mla_projection",
    )(p, p, p, gq, gkv, wq, wkv, cos, sin)


def _attn_kernel(q_ref, kc_ref, vc_ref, *rest, tk, n_lat, heads):
    if n_lat:
        kl_ref, vl_ref, o_ref, m_ref, acc_ref, s_ref = rest
    else:
        o_ref, m_ref, acc_ref = rest
    m_ref[...] = jnp.full_like(m_ref, -1e30)
    acc_ref[...] = jnp.zeros_like(acc_ref)

    def scores(h, k):
        return lax.dot_general(q_ref[h], k, (((1,), (1,)), ((), ())), preferred_element_type=F32)

    def update(h, s, v):
        m_old = m_ref[h]
        m_new = jnp.maximum(m_old[:, 0:1], jnp.max(s, axis=-1, keepdims=True))
        p = jnp.exp2(s - m_new)
        alpha = jnp.exp2(m_old - m_new)
        pv = jnp.dot(p.astype(BF16), v, preferred_element_type=F32)
        acc_ref[h] = jnp.concatenate([alpha, alpha], axis=-1) * acc_ref[h] + pv
        m_ref[h] = jnp.broadcast_to(m_new, m_old.shape)

    def lat(ref, h, j):
        return ref[h, pl.ds(pl.multiple_of(j * tk, tk), tk), :]

    def half(j, cur, nxt):
        for h in range(heads):
            if nxt is not None:
                s_ref[nxt, h] = scores(h, lat(kl_ref, h, j + 1))
            update(h, s_ref[cur, h], lat(vl_ref, h, j))

    ctx_scores = [scores(h, kc_ref[h]) for h in range(heads)]
    if n_lat:
        for h in range(heads):
            s_ref[0, h] = scores(h, lat(kl_ref, h, 0))
    for h in range(heads):
        update(h, ctx_scores[h], vc_ref[h])
    if n_lat:
        def body(jj, carry):
            half(2 * jj, 0, 1)
            half(2 * jj + 1, 1, 0)
            return carry

        lax.fori_loop(0, n_lat // 2 - 1, body, 0)
        half(n_lat - 2, 0, 1)
        half(n_lat - 1, 1, None)
    outs = []
    for h in range(heads):
        acc = acc_ref[h][:, 0:HEAD_PAD] + acc_ref[h][:, HEAD_PAD:2 * HEAD_PAD]
        outs.append(acc[:, 0:MLA_DV] / acc[:, MLA_DV:MLA_DV + 1])
    o_ref[...] = jnp.concatenate(outs, axis=-1).astype(o_ref.dtype)


def _attention(q, k_ctx, v_ctx, k_lat=None, v_lat=None):
    B, H, Tq, _ = q.shape
    Tc = k_ctx.shape[2]
    heads = 2
    tq = min(Tq, 1024)
    tk = 512
    n_lat = 0 if k_lat is None else k_lat.shape[2] // tk
    assert tk % (2 * V_ROWS) == 0 and Tc % V_ROWS == 0
    whole = lambda t, w: pl.BlockSpec((None, heads, t, w), lambda b, g, i: (b, g, 0, 0))
    in_specs = [pl.BlockSpec((None, heads, tq, HEAD_PAD), lambda b, g, i: (b, g, i, 0)),
                whole(Tc, HEAD_PAD), whole(Tc, 2 * HEAD_PAD)]
    args = [q, k_ctx, v_ctx]
    scratch = [pltpu.VMEM((heads, tq, HEAD_PAD), F32), pltpu.VMEM((heads, tq, 2 * HEAD_PAD), F32)]
    if n_lat:
        assert n_lat % 2 == 0
        in_specs += [whole(k_lat.shape[2], HEAD_PAD), whole(k_lat.shape[2], 2 * HEAD_PAD)]
        args += [k_lat, v_lat]
        scratch.append(pltpu.VMEM((2, heads, tq, tk), F32))
    return pl.pallas_call(
        functools.partial(_attn_kernel, tk=tk, n_lat=n_lat, heads=heads),
        grid=(B, H // heads, Tq // tq),
        in_specs=in_specs,
        out_specs=pl.BlockSpec((None, tq, heads * MLA_DV), lambda b, g, i: (b, i, g)),
        out_shape=jax.ShapeDtypeStruct((B, Tq, H * MLA_DV), BF16),
        scratch_shapes=scratch,
        compiler_params=_params("parallel", "parallel", "parallel"),
        name="attention",
    )(*args)


def _outproj_kernel(x_ref, yr_ref, ym_ref, yp_ref, gate_ref, wr_ref, wm_ref, wp_ref, o_ref):
    y = jnp.dot(yr_ref[...], wr_ref[...], preferred_element_type=F32)
    y = y + jnp.dot(ym_ref[...], wm_ref[...], preferred_element_type=F32)
    y = y + jnp.dot(yp_ref[...], wp_ref[...], preferred_element_type=F32)
    o_ref[...] = x_ref[...] + gate_ref[...] * y


def _out_projection(x, y_ret, y_mla, y_pool, mod, row_of_batch, w_r, w_m, w_p):
    B, T, D = x.shape
    tm = min(T, 512)
    rows = lambda w: pl.BlockSpec((None, tm, w), lambda b, i: (b, i, 0))
    const = lambda shape: _resident(shape, lambda b, i: (0, 0))
    return pl.pallas_call(
        _outproj_kernel,
        grid=(B, T // tm),
        in_specs=[
            rows(D), rows(RET_W), rows(MLA_W), rows(POOL_W),
            pl.BlockSpec((None, 1, D), _mod_index(2, row_of_batch)),
            const((RET_W, D)), const((MLA_W, D)), const((POOL_W, D)),
        ],
        out_specs=rows(D),
        out_shape=jax.ShapeDtypeStruct((B, T, D), F32),
        compiler_params=_params("parallel", "parallel"),
        name="out_projection",
    )(x, y_ret, y_mla, y_pool, mod, w_r, w_m, w_p)


def _ffn_kernel(xp_ref, x_ref, xn_ref, g_ref, sh_ref, sc_ref, gate_ref, wup_ref, cw_ref, cb_ref, wd_ref, fg_ref,
                o_ref, h_ref, ua_ref, ub_ref, act_ref, *, tm, fc, final):
    i = pl.program_id(1)
    d = x_ref.shape[-1]
    d_ff = wd_ref.shape[0]
    n_chunks = d_ff // fc

    def norm_mod(x):
        h = _rms(x, d) * g_ref[...]
        return h * (1.0 + sc_ref[...]) + sh_ref[...]

    h_ref[0:HALO, :] = jnp.where(i > 0, norm_mod(xp_ref[...]), 0.0).astype(BF16)
    h_ref[HALO:HALO + tm, :] = norm_mod(x_ref[...]).astype(BF16)
    h_ref[HALO + tm:2 * HALO + tm, :] = jnp.where(i < pl.num_programs(1) - 1, norm_mod(xn_ref[...]), 0.0).astype(BF16)
    rows = tm + 2 * HALO

    def up(u_ref, col, slot):
        u_ref[slot] = jnp.dot(h_ref[...], wup_ref[:, col:col + fc], preferred_element_type=F32)

    def conv(u_ref, slot, col):
        cw = cw_ref[:, col:col + fc]
        u = u_ref[slot]
        y = pltpu.roll(u, 1, axis=0) * cw[0:1] + u * cw[1:2] + pltpu.roll(u, rows - 1, axis=0) * cw[2:3]
        return y[HALO:HALO + tm] + cb_ref[:, col:col + fc]

    up(ua_ref, 0, 0)
    up(ub_ref, d_ff, 0)
    for f in range(n_chunks):
        cur, more = f % 2, f + 1 < n_chunks
        if more:
            up(ua_ref, (f + 1) * fc, 1 - cur)
        a = _silu(conv(ua_ref, cur, f * fc))
        if more:
            up(ub_ref, d_ff + (f + 1) * fc, 1 - cur)
        b = conv(ub_ref, cur, d_ff + f * fc)
        act_ref[:, f * fc:(f + 1) * fc] = (a * b).astype(BF16)
    acc = jnp.dot(act_ref[...], wd_ref[...], preferred_element_type=F32)
    y = x_ref[...] + gate_ref[...] * acc
    if final:
        y = _rms(y, d) * fg_ref[...]
    o_ref[...] = y


def _ffn(x, g, mod, row_of_batch, w_up, conv_w, conv_b, w_down, final_g, final):
    B, T, D = x.shape
    d_ff = w_down.shape[0]
    fc = 256
    assert d_ff % fc == 0
    tm = min(T, 512)
    hb = tm // HALO
    vec = lambda sec: pl.BlockSpec((None, 1, D), _mod_index(sec, row_of_batch))
    const2 = lambda a: _resident(a.shape, lambda b, i: (0, 0))
    return pl.pallas_call(
        functools.partial(_ffn_kernel, tm=tm, fc=fc, final=final),
        grid=(B, T // tm),
        in_specs=[
            pl.BlockSpec((None, HALO, D), lambda b, i: (b, jnp.maximum(i * hb - 1, 0), 0)),
            pl.BlockSpec((None, tm, D), lambda b, i: (b, i, 0)),
            pl.BlockSpec((None, HALO, D), lambda b, i: (b, jnp.minimum((i + 1) * hb, T // HALO - 1), 0)),
            pl.BlockSpec((1, D), lambda b, i: (0, 0)),
            vec(3), vec(4), vec(5),
            const2(w_up), const2(conv_w), const2(conv_b), const2(w_down),
            pl.BlockSpec((1, D), lambda b, i: (0, 0)),
        ],
        out_specs=pl.BlockSpec((None, tm, D), lambda b, i: (b, i, 0)),
        out_shape=jax.ShapeDtypeStruct((B, T, D), F32),
        scratch_shapes=[pltpu.VMEM((tm + 2 * HALO, D), BF16), pltpu.VMEM((2, tm + 2 * HALO, fc), F32),
                        pltpu.VMEM((2, tm + 2 * HALO, fc), F32), pltpu.VMEM((tm, d_ff), BF16)],
        compiler_params=_params("parallel", "parallel"),
        name="conv_ffn",
    )(x, x, x, g.reshape(1, D), mod, mod, mod, w_up, conv_w, conv_b, w_down, final_g.reshape(1, D))


def _swap_halves(w, groups, width):
    lead = w.shape[:-1]
    return jnp.flip(w.reshape(*lead, groups, 2, width // 2), axis=-2).reshape(*lead, groups * width)


def _layout_in_proj(w_in):
    D = w_in.shape[0]
    offs = [0]
    for s in (RET_QK, RET_QK, RET_W, RET_W, MLA_Q_RANK, MLA_KV_RANK, MLA_ROPE, POOL_W):
        offs.append(offs[-1] + s)
    rq, rk, rv, rg, cq, ckv, kr, pool = [w_in[:, offs[j]:offs[j + 1]] for j in range(8)]
    z = lambda n: jnp.zeros((D, n), w_in.dtype)
    place = lambda a: jnp.concatenate([z(MLA_NOPE), a, z(HEAD_PAD - MLA_NOPE - MLA_ROPE)], axis=1)
    cols = [rk, _swap_halves(rk, RET_HEADS, RET_DK), rv, rq, _swap_halves(rq, RET_HEADS, RET_DK), rg,
            cq, z(512 - MLA_Q_RANK), place(kr), place(_swap_halves(kr, 1, MLA_ROPE)), ckv, pool]
    return jnp.concatenate(cols, axis=1).astype(BF16)


def _layout_uq(w_uq):
    r = w_uq.shape[0]
    w = w_uq.reshape(r, MLA_HEADS, MLA_NOPE + MLA_ROPE)
    nope, rope = w[..., :MLA_NOPE], w[..., MLA_NOPE:]
    z = lambda n: jnp.zeros((r, MLA_HEADS, n), w_uq.dtype)
    tail = HEAD_PAD - MLA_NOPE - MLA_ROPE
    per_head = jnp.concatenate([nope, rope, z(tail), z(MLA_NOPE), _swap_halves(rope, 1, MLA_ROPE), z(tail)], axis=-1)
    w = per_head.reshape(r, MLA_HEADS * 2 * HEAD_PAD)
    return jnp.concatenate([w, jnp.zeros((512 - r, w.shape[1]), w.dtype)], axis=0).astype(BF16)


def _layout_ukv(w_ukv):
    r = w_ukv.shape[0]
    w = w_ukv.reshape(r, MLA_HEADS, MLA_NOPE + MLA_DV)
    z = lambda n: jnp.zeros((r, MLA_HEADS, n), w_ukv.dtype)
    per_head = jnp.concatenate([w[..., :MLA_NOPE], z(HEAD_PAD - MLA_NOPE), w[..., MLA_NOPE:], z(HEAD_PAD - MLA_DV)], axis=-1)
    return per_head.reshape(r, MLA_HEADS * 2 * HEAD_PAD).astype(BF16)


def _rope_tables(t):
    rows = t // GRID_W
    row = jnp.repeat(jnp.arange(rows, dtype=F32), GRID_W)
    col = jnp.tile(jnp.arange(GRID_W, dtype=F32), rows)
    n_freq = MLA_ROPE // 4
    inv = ROPE_THETA ** (-jnp.arange(n_freq, dtype=F32) / n_freq)
    ang = jnp.concatenate([row[:, None] * inv, col[:, None] * inv], axis=-1)
    cos, sin = jnp.cos(ang), jnp.sin(ang)
    cc = jnp.concatenate([cos, cos], axis=-1)
    ss = jnp.concatenate([-sin, sin], axis=-1)
    ret_cos, ret_sin = jnp.tile(cc, (1, RET_HEADS)), jnp.tile(ss, (1, RET_HEADS))
    tail = HEAD_PAD - MLA_NOPE - MLA_ROPE
    one, zero = jnp.ones((t, MLA_NOPE), F32), jnp.zeros((t, MLA_NOPE), F32)
    mla_cos = jnp.concatenate([one, cc, jnp.ones((t, tail), F32)], axis=-1)
    mla_sin = jnp.concatenate([zero, ss, jnp.zeros((t, tail), F32)], axis=-1)
    return ret_cos, ret_sin, mla_cos, mla_sin


def kernel(x, c, ctx, c_ctx, w_mod, b_mod, norm1_g, w_in, ret_decay_f, ret_decay_b, mla_q_norm_g, w_uq,
           mla_kv_norm_g, w_ukv, pool_w, pool_scale, w_out, norm2_g, w_up, conv_w, conv_b, w_down,
           final_norm_g):
    B, T, D = x.shape
    L = w_mod.shape[0]
    d_ff = w_down.shape[1]
    assert T % 512 == 0 and ctx.shape[1] % RET_CHUNK == 0 and B + 1 <= SUBLANES

    ret_cos, ret_sin, mla_cos, mla_sin = _rope_tables(T)
    c_rows = jnp.concatenate([c, c_ctx[None, :], jnp.zeros((SUBLANES - B - 1, D), c.dtype)], axis=0)
    mod_all = _modulation(c_rows, w_mod, b_mod)
    lat_row = lambda b: b
    ctx_row = lambda b: B

    xc = ctx
    zero_state = jnp.zeros((B, RET_QK, RET_W), F32)
    for l in range(L):
        with_ctx_out = l < L - 1
        mod = mod_all[l].reshape(SUBLANES * 6, 1, D)
        w_p = _layout_in_proj(w_in[l])
        wq = _layout_uq(w_uq[l])
        wkv = _layout_ukv(w_ukv[l])
        gq = jnp.concatenate([mla_q_norm_g[l], jnp.zeros((512 - MLA_Q_RANK,), F32)]).reshape(1, 512)
        gkv = mla_kv_norm_g[l].reshape(1, MLA_KV_RANK)
        pool_bd = jax.scipy.linalg.block_diag(*[pool_w[l, g] for g in range(len(POOL_WINDOWS))]).astype(BF16)
        wo = w_out[l].astype(BF16)
        wo_r, wo_m, wo_p = wo[:RET_W], wo[RET_W:RET_W + MLA_W], wo[RET_W + MLA_W:]
        ffn_w = (w_up[l].astype(BF16), conv_w[l], conv_b[l].reshape(1, 2 * d_ff), w_down[l].astype(BF16))

        p_lat = _in_projection(x, norm1_g[l], mod, lat_row, w_p)
        p_ctx = _in_projection(xc, norm1_g[l], mod, ctx_row, w_p)

        yr_ctx, s_f, s_b = _retention(p_ctx, ret_cos, ret_sin, ret_decay_f[l], ret_decay_b[l], zero_state, zero_state, False)
        yr_lat, _, _ = _retention(p_lat, ret_cos, ret_sin, ret_decay_f[l], ret_decay_b[l], s_f, s_b, True)

        q_ctx, k_ctx, v_ctx = _mla_projection(p_ctx, gq, gkv, wq, wkv, mla_cos, mla_sin, False)
        q_lat, k_lat, v_lat = _mla_projection(p_lat, gq, gkv, wq, wkv, mla_cos, mla_sin, True)
        ym_lat = _attention(q_lat, k_ctx, v_ctx, k_lat, v_lat)

        yp_lat = _pool(p_lat, pool_bd, pool_scale[l])
        x = _out_projection(x, yr_lat, ym_lat, yp_lat, mod, lat_row, wo_r, wo_m, wo_p)
        x = _ffn(x, norm2_g[l], mod, lat_row, *ffn_w, final_norm_g, not with_ctx_out)
        if with_ctx_out:
            ym_ctx = _attention(q_ctx, k_ctx, v_ctx)
            yp_ctx = _pool(p_ctx, pool_bd, pool_scale[l])
            xc = _out_projection(xc, yr_ctx, ym_ctx, yp_ctx, mod, ctx_row, wo_r, wo_m, wo_p)
            xc = _ffn(xc, norm2_g[l], mod, ctx_row, *ffn_w, final_norm_g, False)
    return x
```

```python
import functools

import jax
import jax.numpy as jnp
from jax import lax
from jax.experimental import pallas as pl
from jax.experimental.pallas import tpu as pltpu

BF16 = jnp.bfloat16
F32 = jnp.float32

GRID_W = 64
ROPE_THETA = 10000.0
NORM_EPS = 1e-6
RET_HEADS = 4
RET_DK = 32
RET_DV = 64
MLA_HEADS = 8
MLA_Q_RANK = 384
MLA_KV_RANK = 256
MLA_NOPE = 64
MLA_ROPE = 32
MLA_DV = 64
POOL_WINDOWS = (2, 4, 8, 16)
POOL_GROUP = 64

LANES = 128
SUBLANES = 8
BF16_ROWS = 16
VMEM_LIMIT_BYTES = 56 * 1024 * 1024

RET_W = RET_HEADS * RET_DV
RET_QK = RET_HEADS * RET_DK
MLA_W = MLA_HEADS * MLA_DV
POOL_W = len(POOL_WINDOWS) * POOL_GROUP
HEAD_PAD = 128
HALO = 16
LOG2_E = 1.4426950408889634
V_ROWS = 256
RET_CHUNK = 128

P_RET = 0
P_CQ = 1024
P_KR = 1536
P_CKV = 1792
P_POOL = 2048
P_WIDTH = 2304


def _params(*semantics):
    return pltpu.CompilerParams(dimension_semantics=semantics, vmem_limit_bytes=VMEM_LIMIT_BYTES)


def _resident(block_shape, index_map):
    return pl.BlockSpec(block_shape, index_map, pipeline_mode=pl.Buffered(1))


def _rms(x, width):
    return x * lax.rsqrt(jnp.sum(x * x, axis=-1, keepdims=True) * (1.0 / width) + NORM_EPS)


def _silu(x):
    return x * jax.nn.sigmoid(x)


def _log_sigmoid(d):
    return jnp.minimum(d, 0.0) - jnp.log1p(jnp.exp(-jnp.abs(d)))


def _mod_kernel(c_ref, w_ref, b_ref, o_ref):
    s = _silu(c_ref[...])
    o_ref[...] = jnp.dot(s.astype(BF16), w_ref[...].astype(BF16), preferred_element_type=F32) + b_ref[...]


def _modulation(c_rows, w_mod, b_mod):
    L, D, W = w_mod.shape
    tn = 1536
    return pl.pallas_call(
        _mod_kernel,
        grid=(L, W // tn),
        in_specs=[
            pl.BlockSpec((SUBLANES, D), lambda l, j: (0, 0)),
            pl.BlockSpec((None, D, tn), lambda l, j: (l, 0, j)),
            pl.BlockSpec((None, 1, tn), lambda l, j: (l, 0, j)),
        ],
        out_specs=pl.BlockSpec((None, SUBLANES, tn), lambda l, j: (l, 0, j)),
        out_shape=jax.ShapeDtypeStruct((L, SUBLANES, W), F32),
        compiler_params=_params("parallel", "parallel"),
        name="modulation",
    )(c_rows, w_mod, b_mod.reshape(L, 1, W))


def _inproj_kernel(x_ref, g_ref, sh_ref, sc_ref, w_ref, o_ref):
    x = x_ref[...]
    h = _rms(x, x.shape[-1]) * g_ref[...]
    h = h * (1.0 + sc_ref[...]) + sh_ref[...]
    o_ref[...] = jnp.dot(h.astype(BF16), w_ref[...], preferred_element_type=F32)


def _mod_index(section, row_of_batch):
    def index(b, *_):
        return (row_of_batch(b) * 6 + section, 0, 0)

    return index


def _in_projection(x, g, mod, row_of_batch, w):
    B, T, D = x.shape
    tm = min(T, 512)
    vec = lambda sec: pl.BlockSpec((None, 1, D), _mod_index(sec, row_of_batch))
    return pl.pallas_call(
        _inproj_kernel,
        grid=(B, T // tm),
        in_specs=[
            pl.BlockSpec((None, tm, D), lambda b, i: (b, i, 0)),
            pl.BlockSpec((1, D), lambda b, i: (0, 0)),
            vec(0),
            vec(1),
            _resident((D, P_WIDTH), lambda b, i: (0, 0)),
        ],
        out_specs=pl.BlockSpec((None, tm, P_WIDTH), lambda b, i: (b, i, 0)),
        out_shape=jax.ShapeDtypeStruct((B, T, P_WIDTH), F32),
        compiler_params=_params("parallel", "parallel"),
        name="in_projection",
    )(x, g.reshape(1, D), mod, mod, w)


def _ret_rope(a, a_swap, cos, sin, rotate):
    if not rotate:
        return a
    return a * cos + a_swap * sin


def _ret_state_kernel(pf_ref, pb_ref, cf_ref, sf_ref, cb_ref, sb_ref, dfl_ref, dbl_ref, dfc_ref, dbc_ref,
                      s0f_ref, s0b_ref, sf_all_ref, sb_all_ref, sf_fin_ref, sb_fin_ref, stf_ref, stb_ref,
                      *, rotate):
    c = RET_CHUNK
    i = pl.program_id(0)

    @pl.when(i == 0)
    def _():
        stf_ref[...] = s0f_ref[...]
        stb_ref[...] = s0b_ref[...]

    sf_all_ref[...] = stf_ref[...]
    sb_all_ref[...] = stb_ref[...]

    n = lax.broadcasted_iota(jnp.int32, (c, RET_QK), 0).astype(F32)
    rows = lax.broadcasted_iota(jnp.int32, (RET_QK, RET_W), 0)
    cols = lax.broadcasted_iota(jnp.int32, (RET_QK, RET_W), 1)
    head_block = (rows // RET_DK) == (cols // RET_DV)

    def update(p_ref, cos_ref, sin_ref, lg_lane, lg_col, pos_pow, st_ref):
        k_decay = jnp.exp(lg_lane * pos_pow) * (RET_DK ** -0.5)
        carry_decay = jnp.exp(lg_col * float(c))
        for b in range(p_ref.shape[0]):
            k = _ret_rope(p_ref[b, :, 0:RET_QK], p_ref[b, :, RET_QK:2 * RET_QK], cos_ref[...], sin_ref[...], rotate)
            v = p_ref[b, :, 2 * RET_QK:2 * RET_QK + RET_W]
            kv = lax.dot_general((k * k_decay).astype(BF16), v.astype(BF16), (((0,), (0,)), ((), ())),
                                 preferred_element_type=F32)
            st_ref[b] = st_ref[b] * carry_decay + jnp.where(head_block, kv, 0.0)

    update(pf_ref, cf_ref, sf_ref, _log_sigmoid(dfl_ref[...]), _log_sigmoid(dfc_ref[...]), (c - 1.0) - n, stf_ref)
    update(pb_ref, cb_ref, sb_ref, _log_sigmoid(dbl_ref[...]), _log_sigmoid(dbc_ref[...]), n, stb_ref)

    @pl.when(i == pl.num_programs(0) - 1)
    def _():
        sf_fin_ref[...] = stf_ref[...]
        sb_fin_ref[...] = stb_ref[...]


def _ret_out_kernel(p_ref, cos_ref, sin_ref, dfl_ref, dbl_ref, dfw_ref, dbw_ref, sf_ref, sb_ref, ones_ref,
                    o_ref, *, rotate, n_sub):
    c = RET_CHUNK
    kr = lax.broadcasted_iota(jnp.int32, (RET_HEADS * c, RET_QK), 0) // c
    kl = lax.broadcasted_iota(jnp.int32, (RET_HEADS * c, RET_QK), 1) // RET_DK
    vr = lax.broadcasted_iota(jnp.int32, (RET_HEADS * c, RET_W), 0) // c
    vl = lax.broadcasted_iota(jnp.int32, (RET_HEADS * c, RET_W), 1) // RET_DV

    lgf_w = _log_sigmoid(dfw_ref[...])
    lgb_w = _log_sigmoid(dbw_ref[...])
    nq = lax.broadcasted_iota(jnp.int32, (c, RET_HEADS * c), 0)
    mk = lax.broadcasted_iota(jnp.int32, (c, RET_HEADS * c), 1) % c
    diff = (nq - mk).astype(F32)
    dmat = jnp.exp(jnp.where(diff >= 0.0, lgf_w, -lgb_w) * diff)
    n = lax.broadcasted_iota(jnp.int32, (c, RET_QK), 0).astype(F32)
    qf_decay = jnp.exp(_log_sigmoid(dfl_ref[...]) * (n + 1.0))
    qb_decay = jnp.exp(_log_sigmoid(dbl_ref[...]) * (float(c) - n))

    for s in range(n_sub):
        r = slice(s * c, (s + 1) * c)
        cos, sin = cos_ref[r, :], sin_ref[r, :]
        k = _ret_rope(p_ref[r, 0:RET_QK], p_ref[r, RET_QK:2 * RET_QK], cos, sin, rotate) * (RET_DK ** -0.5)
        v = p_ref[r, 2 * RET_QK:2 * RET_QK + RET_W]
        q = _ret_rope(p_ref[r, 512:512 + RET_QK], p_ref[r, 512 + RET_QK:512 + 2 * RET_QK], cos, sin, rotate)
        gate = p_ref[r, 768:768 + RET_W]
        k_stack = jnp.where(kr == kl, jnp.tile(k, (RET_HEADS, 1)), 0.0).astype(BF16)
        v_stack = jnp.where(vr == vl, jnp.tile(v, (RET_HEADS, 1)), 0.0).astype(BF16)

        scores = lax.dot_general(q.astype(BF16), k_stack, (((1,), (1,)), ((), ())), preferred_element_type=F32)
        a = (scores * dmat).astype(BF16)
        o = jnp.dot(a, v_stack, preferred_element_type=F32)
        o = o + jnp.dot((q * qf_decay).astype(BF16), sf_ref[s].astype(BF16), preferred_element_type=F32)
        o = o + jnp.dot((q * qb_decay).astype(BF16), sb_ref[s].astype(BF16), preferred_element_type=F32)

        sq = o * o
        hi = sq.astype(BF16)
        lo = (sq - hi.astype(F32)).astype(BF16)
        ssum = (jnp.dot(hi, ones_ref[...], preferred_element_type=F32)
                + jnp.dot(lo, ones_ref[...], preferred_element_type=F32))
        o = o * lax.rsqrt(ssum * (1.0 / RET_DV) + NORM_EPS)
        o_ref[r, :] = (o * _silu(gate)).astype(o_ref.dtype)


def _retention(p, cos, sin, decay_f, decay_b, s0f, s0b, rotate):
    B, T, _ = p.shape
    c = RET_CHUNK
    nc = T // c
    rep = lambda d, r: jnp.repeat(d.astype(F32), r)
    dfl, dbl = rep(decay_f, RET_DK).reshape(1, RET_QK), rep(decay_b, RET_DK).reshape(1, RET_QK)
    dfc, dbc = dfl.reshape(RET_QK, 1), dbl.reshape(RET_QK, 1)
    dfw, dbw = rep(decay_f, c).reshape(1, RET_HEADS * c), rep(decay_b, c).reshape(1, RET_HEADS * c)
    small1 = lambda shape: pl.BlockSpec(shape, lambda i: (0,) * len(shape))
    rope_f = pl.BlockSpec((c, RET_QK), lambda i: (i, 0))
    rope_b = pl.BlockSpec((c, RET_QK), lambda i: (nc - 1 - i, 0))
    state = pl.BlockSpec((B, RET_QK, RET_W), lambda i: (0, 0, 0))

    sf_all, sb_all, sf_fin, sb_fin = pl.pallas_call(
        functools.partial(_ret_state_kernel, rotate=rotate),
        grid=(nc,),
        in_specs=[
            pl.BlockSpec((B, c, 512), lambda i: (0, i, 0)),
            pl.BlockSpec((B, c, 512), lambda i: (0, nc - 1 - i, 0)),
            rope_f, rope_f, rope_b, rope_b,
            small1((1, RET_QK)), small1((1, RET_QK)), small1((RET_QK, 1)), small1((RET_QK, 1)),
            state, state,
        ],
        out_specs=[
            pl.BlockSpec((B, None, RET_QK, RET_W), lambda i: (0, i, 0, 0)),
            pl.BlockSpec((B, None, RET_QK, RET_W), lambda i: (0, nc - 1 - i, 0, 0)),
            state, state,
        ],
        out_shape=[
            jax.ShapeDtypeStruct((B, nc, RET_QK, RET_W), F32),
            jax.ShapeDtypeStruct((B, nc, RET_QK, RET_W), F32),
            jax.ShapeDtypeStruct((B, RET_QK, RET_W), F32),
            jax.ShapeDtypeStruct((B, RET_QK, RET_W), F32),
        ],
        scratch_shapes=[pltpu.VMEM((B, RET_QK, RET_W), F32), pltpu.VMEM((B, RET_QK, RET_W), F32)],
        compiler_params=_params("arbitrary"),
        name="retention_states",
    )(p, p, cos, sin, cos, sin, dfl, dbl, dfc, dbc, s0f, s0b)

    gi = lax.broadcasted_iota(jnp.int32, (RET_W, RET_W), 0) // RET_DV
    gj = lax.broadcasted_iota(jnp.int32, (RET_W, RET_W), 1) // RET_DV
    ones_bd = (gi == gj).astype(BF16)
    n_sub = min(nc, 4)
    tm = n_sub * c
    small = lambda shape: pl.BlockSpec(shape, lambda b, i: (0,) * len(shape))
    rope = pl.BlockSpec((tm, RET_QK), lambda b, i: (i, 0))
    chunk_states = pl.BlockSpec((None, n_sub, RET_QK, RET_W), lambda b, i: (b, i, 0, 0))
    y = pl.pallas_call(
        functools.partial(_ret_out_kernel, rotate=rotate, n_sub=n_sub),
        grid=(B, T // tm),
        in_specs=[
            pl.BlockSpec((None, tm, 1024), lambda b, i: (b, i, 0)),
            rope, rope,
            small((1, RET_QK)), small((1, RET_QK)), small((1, RET_HEADS * c)), small((1, RET_HEADS * c)),
            chunk_states, chunk_states,
            small((RET_W, RET_W)),
        ],
        out_specs=pl.BlockSpec((None, tm, RET_W), lambda b, i: (b, i, 0)),
        out_shape=jax.ShapeDtypeStruct((B, T, RET_W), BF16),
        compiler_params=_params("parallel", "parallel"),
        name="retention_outputs",
    )(p, cos, sin, dfl, dbl, dfw, dbw, sf_all, sb_all, ones_bd)
    return y, sf_fin, sb_fin


def _pool_kernel(prev_ref, x_ref, next_ref, w_ref, scale_ref, o_ref, *, tp, t_total):
    i = pl.program_id(1)
    x = x_ref[...]
    prev = jnp.where(i > 0, prev_ref[...], 0.0)
    nxt = jnp.where(i < pl.num_programs(1) - 1, next_ref[...], 0.0)
    xp = jnp.concatenate([prev, x, nxt], axis=0)
    rows = tp + 2 * HALO
    back = lambda a, s: pltpu.roll(a, s, axis=0)
    ahead = lambda a, s: pltpu.roll(a, rows - s, axis=0)
    a2 = xp + back(xp, 1)
    a4 = a2 + back(a2, 2)
    a8 = a4 + back(a4, 4)
    a16 = a8 + back(a8, 8)
    lane = lax.broadcasted_iota(jnp.int32, (rows, POOL_W), 1)
    group = lane // POOL_GROUP
    win = jnp.where(group == 0, a2, jnp.where(group == 1, ahead(a4, 1), jnp.where(group == 2, ahead(a8, 3), ahead(a16, 7))))
    win = win[HALO:HALO + tp]

    lane = lax.broadcasted_iota(jnp.int32, (tp, POOL_W), 1)
    half = jnp.left_shift(1, lane // POOL_GROUP)
    t = lax.broadcasted_iota(jnp.int32, (tp, POOL_W), 0) + i * tp
    count = jnp.minimum(t + half, t_total) - jnp.maximum(t - half, 0)
    d = win / count.astype(F32) - x
    y = jnp.dot(d.astype(BF16), w_ref[...], preferred_element_type=F32) * scale_ref[...]
    o_ref[...] = y.astype(o_ref.dtype)


def _pool(p, w_bd, scale):
    B, T, _ = p.shape
    tp = min(T, 1024)
    col = P_POOL // POOL_W
    hb = tp // HALO
    return pl.pallas_call(
        functools.partial(_pool_kernel, tp=tp, t_total=T),
        grid=(B, T // tp),
        in_specs=[
            pl.BlockSpec((None, HALO, POOL_W), lambda b, i: (b, jnp.maximum(i * hb - 1, 0), col)),
            pl.BlockSpec((None, tp, POOL_W), lambda b, i: (b, i, col)),
            pl.BlockSpec((None, HALO, POOL_W), lambda b, i: (b, jnp.minimum((i + 1) * hb, T // HALO - 1), col)),
            pl.BlockSpec((POOL_W, POOL_W), lambda b, i: (0, 0)),
            pl.BlockSpec((1, POOL_W), lambda b, i: (0, 0)),
        ],
        out_specs=pl.BlockSpec((None, tp, POOL_W), lambda b, i: (b, i, 0)),
        out_shape=jax.ShapeDtypeStruct((B, T, POOL_W), BF16),
        compiler_params=_params("parallel", "parallel"),
        name="pooling",
    )(p, p, p, w_bd, scale.reshape(1, POOL_W))


def _mla_proj_kernel(cq_ref, kr_ref, ckv_ref, gq_ref, gkv_ref, wq_ref, wkv_ref, cos_ref, sin_ref,
                     q_ref, k_ref, v_ref, *, rotate):
    scale = (MLA_NOPE + MLA_ROPE) ** -0.5 * LOG2_E
    cq = _rms(cq_ref[...], MLA_Q_RANK) * gq_ref[...]
    q = jnp.dot(cq.astype(BF16), wq_ref[...], preferred_element_type=F32)
    ckv = _rms(ckv_ref[...], MLA_KV_RANK) * gkv_ref[...]
    kv = jnp.dot(ckv.astype(BF16), wkv_ref[...], preferred_element_type=F32)
    kr = kr_ref[:, 0:HEAD_PAD]
    if rotate:
        kr = kr * cos_ref[...] + kr_ref[:, HEAD_PAD:2 * HEAD_PAD] * sin_ref[...]
    ones_col = (lax.broadcasted_iota(jnp.int32, (1, HEAD_PAD), 1) == MLA_DV).astype(F32)
    left = (pl.program_id(1) % 2 == 0).astype(F32)
    for h in range(MLA_HEADS):
        base = 2 * HEAD_PAD * h
        qh = q[:, base:base + HEAD_PAD]
        if rotate:
            qh = qh * cos_ref[...] + q[:, base + HEAD_PAD:base + 2 * HEAD_PAD] * sin_ref[...]
        q_ref[h] = (qh * scale).astype(BF16)
        k_ref[h] = (kv[:, base:base + HEAD_PAD] + kr).astype(BF16)
        vh = kv[:, base + HEAD_PAD:base + 2 * HEAD_PAD] + ones_col
        v_ref[h] = jnp.concatenate([vh * left, vh * (1.0 - left)], axis=-1).astype(BF16)


def _mla_projection(p, gq, gkv, wq, wkv, cos, sin, rotate):
    B, T, _ = p.shape
    tm = V_ROWS
    heads = pl.BlockSpec((None, MLA_HEADS, tm, HEAD_PAD), lambda b, i: (b, 0, i, 0))
    shape = jax.ShapeDtypeStruct((B, MLA_HEADS, T, HEAD_PAD), BF16)
    v_heads = pl.BlockSpec((None, MLA_HEADS, tm, 2 * HEAD_PAD), lambda b, i: (b, 0, i, 0))
    v_shape = jax.ShapeDtypeStruct((B, MLA_HEADS, T, 2 * HEAD_PAD), BF16)
    const = lambda shape_: _resident(shape_, lambda b, i: (0, 0))
    return pl.pallas_call(
        functools.partial(_mla_proj_kernel, rotate=rotate),
        grid=(B, T // tm),
        in_specs=[
            pl.BlockSpec((None, tm, 512), lambda b, i: (b, i, P_CQ // 512)),
            pl.BlockSpec((None, tm, 256), lambda b, i: (b, i, P_KR // 256)),
            pl.BlockSpec((None, tm, 256), lambda b, i: (b, i, P_CKV // 256)),
            const((1, 512)), const((1, 256)), const((512, 2 * HEAD_PAD * MLA_HEADS)),
            const((256, 2 * HEAD_PAD * MLA_HEADS)),
            pl.BlockSpec((tm, HEAD_PAD), lambda b, i: (i, 0)),
            pl.BlockSpec((tm, HEAD_PAD), lambda b, i: (i, 0)),
        ],
        out_specs=[heads, heads, v_heads],
        out_shape=[shape, shape, v_shape],
        compiler_params=_params("parallel", "parallel"),
        name="mla_projection",
    )(p, p, p, gq, gkv, wq, wkv, cos, sin)


def _attn_kernel(q_ref, kc_ref, vc_ref, *rest, tk, n_lat, heads):
    if n_lat:
        kl_ref, vl_ref, o_ref, m_ref, acc_ref, s_ref = rest
    else:
        o_ref, m_ref, acc_ref = rest
    m_ref[...] = jnp.full_like(m_ref, -1e30)
    acc_ref[...] = jnp.zeros_like(acc_ref)

    def scores(h, k):
        return lax.dot_general(q_ref[h], k, (((1,), (1,)), ((), ())), preferred_element_type=F32)

    def update(h, s, v):
        m_old = m_ref[h]
        m_new = jnp.maximum(m_old[:, 0:1], jnp.max(s, axis=-1, keepdims=True))
        p = jnp.exp2(s - m_new)
        alpha = jnp.exp2(m_old - m_new)
        pv = jnp.dot(p.astype(BF16), v, preferred_element_type=F32)
        acc_ref[h] = jnp.concatenate([alpha, alpha], axis=-1) * acc_ref[h] + pv
        m_ref[h] = jnp.broadcast_to(m_new, m_old.shape)

    def lat(ref, h, j):
        return ref[h, pl.ds(pl.multiple_of(j * tk, tk), tk), :]

    def half(j, cur, nxt):
        for h in range(heads):
            if nxt is not None:
                s_ref[nxt, h] = scores(h, lat(kl_ref, h, j + 1))
            update(h, s_ref[cur, h], lat(vl_ref, h, j))

    ctx_scores = [scores(h, kc_ref[h]) for h in range(heads)]
    if n_lat:
        for h in range(heads):
            s_ref[0, h] = scores(h, lat(kl_ref, h, 0))
    for h in range(heads):
        update(h, ctx_scores[h], vc_ref[h])
    if n_lat:
        def body(jj, carry):
            half(2 * jj, 0, 1)
            half(2 * jj + 1, 1, 0)
            return carry

        lax.fori_loop(0, n_lat // 2 - 1, body, 0)
        half(n_lat - 2, 0, 1)
        half(n_lat - 1, 1, None)
    outs = []
    for h in range(heads):
        acc = acc_ref[h][:, 0:HEAD_PAD] + acc_ref[h][:, HEAD_PAD:2 * HEAD_PAD]
        outs.append(acc[:, 0:MLA_DV] / acc[:, MLA_DV:MLA_DV + 1])
    o_ref[...] = jnp.concatenate(outs, axis=-1).astype(o_ref.dtype)


def _attention(q, k_ctx, v_ctx, k_lat=None, v_lat=None):
    B, H, Tq, _ = q.shape
    Tc = k_ctx.shape[2]
    heads = 2
    tq = min(Tq, 1024)
    tk = 1024
    n_lat = 0 if k_lat is None else k_lat.shape[2] // tk
    assert tk % (2 * V_ROWS) == 0 and Tc % V_ROWS == 0
    whole = lambda t, w: pl.BlockSpec((None, heads, t, w), lambda b, g, i: (b, g, 0, 0))
    in_specs = [pl.BlockSpec((None, heads, tq, HEAD_PAD), lambda b, g, i: (b, g, i, 0)),
                whole(Tc, HEAD_PAD), whole(Tc, 2 * HEAD_PAD)]
    args = [q, k_ctx, v_ctx]
    scratch = [pltpu.VMEM((heads, tq, HEAD_PAD), F32), pltpu.VMEM((heads, tq, 2 * HEAD_PAD), F32)]
    if n_lat:
        assert n_lat % 2 == 0
        in_specs += [whole(k_lat.shape[2], HEAD_PAD), whole(k_lat.shape[2], 2 * HEAD_PAD)]
        args += [k_lat, v_lat]
        scratch.append(pltpu.VMEM((2, heads, tq, tk), F32))
    return pl.pallas_call(
        functools.partial(_attn_kernel, tk=tk, n_lat=n_lat, heads=heads),
        grid=(B, H // heads, Tq // tq),
        in_specs=in_specs,
        out_specs=pl.BlockSpec((None, tq, heads * MLA_DV), lambda b, g, i: (b, i, g)),
        out_shape=jax.ShapeDtypeStruct((B, Tq, H * MLA_DV), BF16),
        scratch_shapes=scratch,
        compiler_params=_params("parallel", "parallel", "parallel"),
        name="attention",
    )(*args)


def _outproj_kernel(x_ref, yr_ref, ym_ref, yp_ref, gate_ref, wr_ref, wm_ref, wp_ref, o_ref):
    y = jnp.dot(yr_ref[...], wr_ref[...], preferred_element_type=F32)
    y = y + jnp.dot(ym_ref[...], wm_ref[...], preferred_element_type=F32)
    y = y + jnp.dot(yp_ref[...], wp_ref[...], preferred_element_type=F32)
    o_ref[...] = x_ref[...] + gate_ref[...] * y


def _out_projection(x, y_ret, y_mla, y_pool, mod, row_of_batch, w_r, w_m, w_p):
    B, T, D = x.shape
    tm = min(T, 512)
    rows = lambda w: pl.BlockSpec((None, tm, w), lambda b, i: (b, i, 0))
    const = lambda shape: _resident(shape, lambda b, i: (0, 0))
    return pl.pallas_call(
        _outproj_kernel,
        grid=(B, T // tm),
        in_specs=[
            rows(D), rows(RET_W), rows(MLA_W), rows(POOL_W),
            pl.BlockSpec((None, 1, D), _mod_index(2, row_of_batch)),
            const((RET_W, D)), const((MLA_W, D)), const((POOL_W, D)),
        ],
        out_specs=rows(D),
        out_shape=jax.ShapeDtypeStruct((B, T, D), F32),
        compiler_params=_params("parallel", "parallel"),
        name="out_projection",
    )(x, y_ret, y_mla, y_pool, mod, w_r, w_m, w_p)


def _ffn_kernel(xp_ref, x_ref, xn_ref, g_ref, sh_ref, sc_ref, gate_ref, wup_ref, cw_ref, cb_ref, wd_ref, fg_ref,
                o_ref, h_ref, ua_ref, ub_ref, act_ref, *, tm, fc, final):
    i = pl.program_id(1)
    d = x_ref.shape[-1]
    d_ff = wd_ref.shape[0]
    n_chunks = d_ff // fc

    def norm_mod(x):
        h = _rms(x, d) * g_ref[...]
        return h * (1.0 + sc_ref[...]) + sh_ref[...]

    h_ref[0:HALO, :] = jnp.where(i > 0, norm_mod(xp_ref[...]), 0.0).astype(BF16)
    h_ref[HALO:HALO + tm, :] = norm_mod(x_ref[...]).astype(BF16)
    h_ref[HALO + tm:2 * HALO + tm, :] = jnp.where(i < pl.num_programs(1) - 1, norm_mod(xn_ref[...]), 0.0).astype(BF16)
    rows = tm + 2 * HALO

    def up(u_ref, col, slot):
        u_ref[slot] = jnp.dot(h_ref[...], wup_ref[:, col:col + fc], preferred_element_type=F32)

    def conv(u_ref, slot, col):
        cw = cw_ref[:, col:col + fc]
        u = u_ref[slot]
        y = pltpu.roll(u, 1, axis=0) * cw[0:1] + u * cw[1:2] + pltpu.roll(u, rows - 1, axis=0) * cw[2:3]
        return y[HALO:HALO + tm] + cb_ref[:, col:col + fc]

    up(ua_ref, 0, 0)
    up(ub_ref, d_ff, 0)
    for f in range(n_chunks):
        cur, more = f % 2, f + 1 < n_chunks
        if more:
            up(ua_ref, (f + 1) * fc, 1 - cur)
        a = _silu(conv(ua_ref, cur, f * fc))
        if more:
            up(ub_ref, d_ff + (f + 1) * fc, 1 - cur)
        b = conv(ub_ref, cur, d_ff + f * fc)
        act_ref[:, f * fc:(f + 1) * fc] = (a * b).astype(BF16)
    acc = jnp.dot(act_ref[...], wd_ref[...], preferred_element_type=F32)
    y = x_ref[...] + gate_ref[...] * acc
    if final:
        y = _rms(y, d) * fg_ref[...]
    o_ref[...] = y


def _ffn(x, g, mod, row_of_batch, w_up, conv_w, conv_b, w_down, final_g, final):
    B, T, D = x.shape
    d_ff = w_down.shape[0]
    fc = 256
    assert d_ff % fc == 0
    tm = min(T, 512)
    hb = tm // HALO
    vec = lambda sec: pl.BlockSpec((None, 1, D), _mod_index(sec, row_of_batch))
    const2 = lambda a: _resident(a.shape, lambda b, i: (0, 0))
    return pl.pallas_call(
        functools.partial(_ffn_kernel, tm=tm, fc=fc, final=final),
        grid=(B, T // tm),
        in_specs=[
            pl.BlockSpec((None, HALO, D), lambda b, i: (b, jnp.maximum(i * hb - 1, 0), 0)),
            pl.BlockSpec((None, tm, D), lambda b, i: (b, i, 0)),
            pl.BlockSpec((None, HALO, D), lambda b, i: (b, jnp.minimum((i + 1) * hb, T // HALO - 1), 0)),
            pl.BlockSpec((1, D), lambda b, i: (0, 0)),
            vec(3), vec(4), vec(5),
            const2(w_up), const2(conv_w), const2(conv_b), const2(w_down),
            pl.BlockSpec((1, D), lambda b, i: (0, 0)),
        ],
        out_specs=pl.BlockSpec((None, tm, D), lambda b, i: (b, i, 0)),
        out_shape=jax.ShapeDtypeStruct((B, T, D), F32),
        scratch_shapes=[pltpu.VMEM((tm + 2 * HALO, D), BF16), pltpu.VMEM((2, tm + 2 * HALO, fc), F32),
                        pltpu.VMEM((2, tm + 2 * HALO, fc), F32), pltpu.VMEM((tm, d_ff), BF16)],
        compiler_params=_params("parallel", "parallel"),
        name="conv_ffn",
    )(x, x, x, g.reshape(1, D), mod, mod, mod, w_up, conv_w, conv_b, w_down, final_g.reshape(1, D))


def _swap_halves(w, groups, width):
    lead = w.shape[:-1]
    return jnp.flip(w.reshape(*lead, groups, 2, width // 2), axis=-2).reshape(*lead, groups * width)


def _layout_in_proj(w_in):
    D = w_in.shape[0]
    offs = [0]
    for s in (RET_QK, RET_QK, RET_W, RET_W, MLA_Q_RANK, MLA_KV_RANK, MLA_ROPE, POOL_W):
        offs.append(offs[-1] + s)
    rq, rk, rv, rg, cq, ckv, kr, pool = [w_in[:, offs[j]:offs[j + 1]] for j in range(8)]
    z = lambda n: jnp.zeros((D, n), w_in.dtype)
    place = lambda a: jnp.concatenate([z(MLA_NOPE), a, z(HEAD_PAD - MLA_NOPE - MLA_ROPE)], axis=1)
    cols = [rk, _swap_halves(rk, RET_HEADS, RET_DK), rv, rq, _swap_halves(rq, RET_HEADS, RET_DK), rg,
            cq, z(512 - MLA_Q_RANK), place(kr), place(_swap_halves(kr, 1, MLA_ROPE)), ckv, pool]
    return jnp.concatenate(cols, axis=1).astype(BF16)


def _layout_uq(w_uq):
    r = w_uq.shape[0]
    w = w_uq.reshape(r, MLA_HEADS, MLA_NOPE + MLA_ROPE)
    nope, rope = w[..., :MLA_NOPE], w[..., MLA_NOPE:]
    z = lambda n: jnp.zeros((r, MLA_HEADS, n), w_uq.dtype)
    tail = HEAD_PAD - MLA_NOPE - MLA_ROPE
    per_head = jnp.concatenate([nope, rope, z(tail), z(MLA_NOPE), _swap_halves(rope, 1, MLA_ROPE), z(tail)], axis=-1)
    w = per_head.reshape(r, MLA_HEADS * 2 * HEAD_PAD)
    return jnp.concatenate([w, jnp.zeros((512 - r, w.shape[1]), w.dtype)], axis=0).astype(BF16)


def _layout_ukv(w_ukv):
    r = w_ukv.shape[0]
    w = w_ukv.reshape(r, MLA_HEADS, MLA_NOPE + MLA_DV)
    z = lambda n: jnp.zeros((r, MLA_HEADS, n), w_ukv.dtype)
    per_head = jnp.concatenate([w[..., :MLA_NOPE], z(HEAD_PAD - MLA_NOPE), w[..., MLA_NOPE:], z(HEAD_PAD - MLA_DV)], axis=-1)
    return per_head.reshape(r, MLA_HEADS * 2 * HEAD_PAD).astype(BF16)


def _rope_tables(t):
    rows = t // GRID_W
    row = jnp.repeat(jnp.arange(rows, dtype=F32), GRID_W)
    col = jnp.tile(jnp.arange(GRID_W, dtype=F32), rows)
    n_freq = MLA_ROPE // 4
    inv = ROPE_THETA ** (-jnp.arange(n_freq, dtype=F32) / n_freq)
    ang = jnp.concatenate([row[:, None] * inv, col[:, None] * inv], axis=-1)
    cos, sin = jnp.cos(ang), jnp.sin(ang)
    cc = jnp.concatenate([cos, cos], axis=-1)
    ss = jnp.concatenate([-sin, sin], axis=-1)
    ret_cos, ret_sin = jnp.tile(cc, (1, RET_HEADS)), jnp.tile(ss, (1, RET_HEADS))
    tail = HEAD_PAD - MLA_NOPE - MLA_ROPE
    one, zero = jnp.ones((t, MLA_NOPE), F32), jnp.zeros((t, MLA_NOPE), F32)
    mla_cos = jnp.concatenate([one, cc, jnp.ones((t, tail), F32)], axis=-1)
    mla_sin = jnp.concatenate([zero, ss, jnp.zeros((t, tail), F32)], axis=-1)
    return ret_cos, ret_sin, mla_cos, mla_sin


def kernel(x, c, ctx, c_ctx, w_mod, b_mod, norm1_g, w_in, ret_decay_f, ret_decay_b, mla_q_norm_g, w_uq,
           mla_kv_norm_g, w_ukv, pool_w, pool_scale, w_out, norm2_g, w_up, conv_w, conv_b, w_down,
           final_norm_g):
    B, T, D = x.shape
    L = w_mod.shape[0]
    d_ff = w_down.shape[1]
    assert T % 512 == 0 and ctx.shape[1] % RET_CHUNK == 0 and B + 1 <= SUBLANES

    ret_cos, ret_sin, mla_cos, mla_sin = _rope_tables(T)
    c_rows = jnp.concatenate([c, c_ctx[None, :], jnp.zeros((SUBLANES - B - 1, D), c.dtype)], axis=0)
    mod_all = _modulation(c_rows, w_mod, b_mod)
    lat_row = lambda b: b
    ctx_row = lambda b: B

    xc = ctx
    zero_state = jnp.zeros((B, RET_QK, RET_W), F32)
    for l in range(L):
        with_ctx_out = l < L - 1
        mod = mod_all[l].reshape(SUBLANES * 6, 1, D)
        w_p = _layout_in_proj(w_in[l])
        wq = _layout_uq(w_uq[l])
        wkv = _layout_ukv(w_ukv[l])
        gq = jnp.concatenate([mla_q_norm_g[l], jnp.zeros((512 - MLA_Q_RANK,), F32)]).reshape(1, 512)
        gkv = mla_kv_norm_g[l].reshape(1, MLA_KV_RANK)
        pool_bd = jax.scipy.linalg.block_diag(*[pool_w[l, g] for g in range(len(POOL_WINDOWS))]).astype(BF16)
        wo = w_out[l].astype(BF16)
        wo_r, wo_m, wo_p = wo[:RET_W], wo[RET_W:RET_W + MLA_W], wo[RET_W + MLA_W:]
        ffn_w = (w_up[l].astype(BF16), conv_w[l], conv_b[l].reshape(1, 2 * d_ff), w_down[l].astype(BF16))

        p_lat = _in_projection(x, norm1_g[l], mod, lat_row, w_p)
        p_ctx = _in_projection(xc, norm1_g[l], mod, ctx_row, w_p)

        yr_ctx, s_f, s_b = _retention(p_ctx, ret_cos, ret_sin, ret_decay_f[l], ret_decay_b[l], zero_state, zero_state, False)
        yr_lat, _, _ = _retention(p_lat, ret_cos, ret_sin, ret_decay_f[l], ret_decay_b[l], s_f, s_b, True)

        q_ctx, k_ctx, v_ctx = _mla_projection(p_ctx, gq, gkv, wq, wkv, mla_cos, mla_sin, False)
        q_lat, k_lat, v_lat = _mla_projection(p_lat, gq, gkv, wq, wkv, mla_cos, mla_sin, True)
        ym_lat = _attention(q_lat, k_ctx, v_ctx, k_lat, v_lat)

        yp_lat = _pool(p_lat, pool_bd, pool_scale[l])
        x = _out_projection(x, yr_lat, ym_lat, yp_lat, mod, lat_row, wo_r, wo_m, wo_p)
        x = _ffn(x, norm2_g[l], mod, lat_row, *ffn_w, final_norm_g, not with_ctx_out)
        if with_ctx_out:
            ym_ctx = _attention(q_ctx, k_ctx, v_ctx)
            yp_ctx = _pool(p_ctx, pool_bd, pool_scale[l])
            xc = _out_projection(xc, yr_ctx, ym_ctx, yp_ctx, mod, ctx_row, wo_r, wo_m, wo_p)
            xc = _ffn(xc, norm2_g[l], mod, ctx_row, *ffn_w, final_norm_g, False)
    return x
```

```python
import functools

import jax
import jax.numpy as jnp
from jax import lax
from jax.experimental import pallas as pl
from jax.experimental.pallas import tpu as pltpu

BF16 = jnp.bfloat16
F32 = jnp.float32

GRID_W = 64
ROPE_THETA = 10000.0
NORM_EPS = 1e-6
RET_HEADS = 4
RET_DK = 32
RET_DV = 64
MLA_HEADS = 8
MLA_Q_RANK = 384
MLA_KV_RANK = 256
MLA_NOPE = 64
MLA_ROPE = 32
MLA_DV = 64
POOL_WINDOWS = (2, 4, 8, 16)
POOL_GROUP = 64

LANES = 128
SUBLANES = 8
BF16_ROWS = 16
VMEM_LIMIT_BYTES = 56 * 1024 * 1024

RET_W = RET_HEADS * RET_DV
RET_QK = RET_HEADS * RET_DK
MLA_W = MLA_HEADS * MLA_DV
POOL_W = len(POOL_WINDOWS) * POOL_GROUP
HEAD_PAD = 128
HALO = 16
LOG2_E = 1.4426950408889634
V_ROWS = 256
RET_CHUNK = 128

P_RET = 0
P_CQ = 1024
P_KR = 1536
P_CKV = 1792
P_POOL = 2048
P_WIDTH = 2304


def _params(*semantics):
    return pltpu.CompilerParams(dimension_semantics=semantics, vmem_limit_bytes=VMEM_LIMIT_BYTES)


def _resident(block_shape, index_map):
    return pl.BlockSpec(block_shape, index_map, pipeline_mode=pl.Buffered(1))


def _rms(x, width):
    return x * lax.rsqrt(jnp.sum(x * x, axis=-1, keepdims=True) * (1.0 / width) + NORM_EPS)


def _silu(x):
    return x * jax.nn.sigmoid(x)


def _log_sigmoid(d):
    return jnp.minimum(d, 0.0) - jnp.log1p(jnp.exp(-jnp.abs(d)))


def _mod_kernel(c_ref, w_ref, b_ref, o_ref):
    s = _silu(c_ref[...])
    o_ref[...] = jnp.dot(s.astype(BF16), w_ref[...].astype(BF16), preferred_element_type=F32) + b_ref[...]


def _modulation(c_rows, w_mod, b_mod):
    L, D, W = w_mod.shape
    tn = 1536
    return pl.pallas_call(
        _mod_kernel,
        grid=(L, W // tn),
        in_specs=[
            pl.BlockSpec((SUBLANES, D), lambda l, j: (0, 0)),
            pl.BlockSpec((None, D, tn), lambda l, j: (l, 0, j)),
            pl.BlockSpec((None, 1, tn), lambda l, j: (l, 0, j)),
        ],
        out_specs=pl.BlockSpec((None, SUBLANES, tn), lambda l, j: (l, 0, j)),
        out_shape=jax.ShapeDtypeStruct((L, SUBLANES, W), F32),
        compiler_params=_params("parallel", "parallel"),
        name="modulation",
    )(c_rows, w_mod, b_mod.reshape(L, 1, W))


def _inproj_kernel(x_ref, g_ref, sh_ref, sc_ref, w_ref, o_ref):
    x = x_ref[...]
    h = _rms(x, x.shape[-1]) * g_ref[...]
    h = h * (1.0 + sc_ref[...]) + sh_ref[...]
    o_ref[...] = jnp.dot(h.astype(BF16), w_ref[...], preferred_element_type=F32)


def _mod_index(section, row_of_batch):
    def index(b, *_):
        return (row_of_batch(b) * 6 + section, 0, 0)

    return index


def _in_projection(x, g, mod, row_of_batch, w):
    B, T, D = x.shape
    tm = min(T, 512)
    vec = lambda sec: pl.BlockSpec((None, 1, D), _mod_index(sec, row_of_batch))
    return pl.pallas_call(
        _inproj_kernel,
        grid=(B, T // tm),
        in_specs=[
            pl.BlockSpec((None, tm, D), lambda b, i: (b, i, 0)),
            pl.BlockSpec((1, D), lambda b, i: (0, 0)),
            vec(0),
            vec(1),
            _resident((D, P_WIDTH), lambda b, i: (0, 0)),
        ],
        out_specs=pl.BlockSpec((None, tm, P_WIDTH), lambda b, i: (b, i, 0)),
        out_shape=jax.ShapeDtypeStruct((B, T, P_WIDTH), F32),
        compiler_params=_params("parallel", "parallel"),
        name="in_projection",
    )(x, g.reshape(1, D), mod, mod, w)


def _ret_rope(a, a_swap, cos, sin, rotate):
    if not rotate:
        return a
    return a * cos + a_swap * sin


def _ret_state_kernel(pf_ref, pb_ref, cf_ref, sf_ref, cb_ref, sb_ref, dfl_ref, dbl_ref, dfc_ref, dbc_ref,
                      s0f_ref, s0b_ref, sf_all_ref, sb_all_ref, sf_fin_ref, sb_fin_ref, stf_ref, stb_ref,
                      *, rotate):
    c = RET_CHUNK
    i = pl.program_id(0)

    @pl.when(i == 0)
    def _():
        stf_ref[...] = s0f_ref[...]
        stb_ref[...] = s0b_ref[...]

    sf_all_ref[...] = stf_ref[...]
    sb_all_ref[...] = stb_ref[...]

    n = lax.broadcasted_iota(jnp.int32, (c, RET_QK), 0).astype(F32)
    rows = lax.broadcasted_iota(jnp.int32, (RET_QK, RET_W), 0)
    cols = lax.broadcasted_iota(jnp.int32, (RET_QK, RET_W), 1)
    head_block = (rows // RET_DK) == (cols // RET_DV)

    def update(p_ref, cos_ref, sin_ref, lg_lane, lg_col, pos_pow, st_ref):
        k_decay = jnp.exp(lg_lane * pos_pow) * (RET_DK ** -0.5)
        carry_decay = jnp.exp(lg_col * float(c))
        for b in range(p_ref.shape[0]):
            k = _ret_rope(p_ref[b, :, 0:RET_QK], p_ref[b, :, RET_QK:2 * RET_QK], cos_ref[...], sin_ref[...], rotate)
            v = p_ref[b, :, 2 * RET_QK:2 * RET_QK + RET_W]
            kv = lax.dot_general((k * k_decay).astype(BF16), v.astype(BF16), (((0,), (0,)), ((), ())),
                                 preferred_element_type=F32)
            st_ref[b] = st_ref[b] * carry_decay + jnp.where(head_block, kv, 0.0)

    update(pf_ref, cf_ref, sf_ref, _log_sigmoid(dfl_ref[...]), _log_sigmoid(dfc_ref[...]), (c - 1.0) - n, stf_ref)
    update(pb_ref, cb_ref, sb_ref, _log_sigmoid(dbl_ref[...]), _log_sigmoid(dbc_ref[...]), n, stb_ref)

    @pl.when(i == pl.num_programs(0) - 1)
    def _():
        sf_fin_ref[...] = stf_ref[...]
        sb_fin_ref[...] = stb_ref[...]


def _ret_out_kernel(p_ref, cos_ref, sin_ref, dfl_ref, dbl_ref, dfw_ref, dbw_ref, sf_ref, sb_ref, ones_ref,
                    o_ref, *, rotate, n_sub):
    c = RET_CHUNK
    kr = lax.broadcasted_iota(jnp.int32, (RET_HEADS * c, RET_QK), 0) // c
    kl = lax.broadcasted_iota(jnp.int32, (RET_HEADS * c, RET_QK), 1) // RET_DK
    vr = lax.broadcasted_iota(jnp.int32, (RET_HEADS * c, RET_W), 0) // c
    vl = lax.broadcasted_iota(jnp.int32, (RET_HEADS * c, RET_W), 1) // RET_DV

    lgf_w = _log_sigmoid(dfw_ref[...])
    lgb_w = _log_sigmoid(dbw_ref[...])
    nq = lax.broadcasted_iota(jnp.int32, (c, RET_HEADS * c), 0)
    mk = lax.broadcasted_iota(jnp.int32, (c, RET_HEADS * c), 1) % c
    diff = (nq - mk).astype(F32)
    dmat = jnp.exp(jnp.where(diff >= 0.0, lgf_w, -lgb_w) * diff)
    n = lax.broadcasted_iota(jnp.int32, (c, RET_QK), 0).astype(F32)
    qf_decay = jnp.exp(_log_sigmoid(dfl_ref[...]) * (n + 1.0))
    qb_decay = jnp.exp(_log_sigmoid(dbl_ref[...]) * (float(c) - n))

    for s in range(n_sub):
        r = slice(s * c, (s + 1) * c)
        cos, sin = cos_ref[r, :], sin_ref[r, :]
        k = _ret_rope(p_ref[r, 0:RET_QK], p_ref[r, RET_QK:2 * RET_QK], cos, sin, rotate) * (RET_DK ** -0.5)
        v = p_ref[r, 2 * RET_QK:2 * RET_QK + RET_W]
        q = _ret_rope(p_ref[r, 512:512 + RET_QK], p_ref[r, 512 + RET_QK:512 + 2 * RET_QK], cos, sin, rotate)
        gate = p_ref[r, 768:768 + RET_W]
        k_stack = jnp.where(kr == kl, jnp.tile(k, (RET_HEADS, 1)), 0.0).astype(BF16)
        v_stack = jnp.where(vr == vl, jnp.tile(v, (RET_HEADS, 1)), 0.0).astype(BF16)

        scores = lax.dot_general(q.astype(BF16), k_stack, (((1,), (1,)), ((), ())), preferred_element_type=F32)
        a = (scores * dmat).astype(BF16)
        o = jnp.dot(a, v_stack, preferred_element_type=F32)
        o = o + jnp.dot((q * qf_decay).astype(BF16), sf_ref[s].astype(BF16), preferred_element_type=F32)
        o = o + jnp.dot((q * qb_decay).astype(BF16), sb_ref[s].astype(BF16), preferred_element_type=F32)

        sq = o * o
        hi = sq.astype(BF16)
        lo = (sq - hi.astype(F32)).astype(BF16)
        ssum = (jnp.dot(hi, ones_ref[...], preferred_element_type=F32)
                + jnp.dot(lo, ones_ref[...], preferred_element_type=F32))
        o = o * lax.rsqrt(ssum * (1.0 / RET_DV) + NORM_EPS)
        o_ref[r, :] = (o * _silu(gate)).astype(o_ref.dtype)


def _retention(p, cos, sin, decay_f, decay_b, s0f, s0b, rotate):
    B, T, _ = p.shape
    c = RET_CHUNK
    nc = T // c
    rep = lambda d, r: jnp.repeat(d.astype(F32), r)
    dfl, dbl = rep(decay_f, RET_DK).reshape(1, RET_QK), rep(decay_b, RET_DK).reshape(1, RET_QK)
    dfc, dbc = dfl.reshape(RET_QK, 1), dbl.reshape(RET_QK, 1)
    dfw, dbw = rep(decay_f, c).reshape(1, RET_HEADS * c), rep(decay_b, c).reshape(1, RET_HEADS * c)
    small1 = lambda shape: pl.BlockSpec(shape, lambda i: (0,) * len(shape))
    rope_f = pl.BlockSpec((c, RET_QK), lambda i: (i, 0))
    rope_b = pl.BlockSpec((c, RET_QK), lambda i: (nc - 1 - i, 0))
    state = pl.BlockSpec((B, RET_QK, RET_W), lambda i: (0, 0, 0))

    sf_all, sb_all, sf_fin, sb_fin = pl.pallas_call(
        functools.partial(_ret_state_kernel, rotate=rotate),
        grid=(nc,),
        in_specs=[
            pl.BlockSpec((B, c, 512), lambda i: (0, i, 0)),
            pl.BlockSpec((B, c, 512), lambda i: (0, nc - 1 - i, 0)),
            rope_f, rope_f, rope_b, rope_b,
            small1((1, RET_QK)), small1((1, RET_QK)), small1((RET_QK, 1)), small1((RET_QK, 1)),
            state, state,
        ],
        out_specs=[
            pl.BlockSpec((B, None, RET_QK, RET_W), lambda i: (0, i, 0, 0)),
            pl.BlockSpec((B, None, RET_QK, RET_W), lambda i: (0, nc - 1 - i, 0, 0)),
            state, state,
        ],
        out_shape=[
            jax.ShapeDtypeStruct((B, nc, RET_QK, RET_W), F32),
            jax.ShapeDtypeStruct((B, nc, RET_QK, RET_W), F32),
            jax.ShapeDtypeStruct((B, RET_QK, RET_W), F32),
            jax.ShapeDtypeStruct((B, RET_QK, RET_W), F32),
        ],
        scratch_shapes=[pltpu.VMEM((B, RET_QK, RET_W), F32), pltpu.VMEM((B, RET_QK, RET_W), F32)],
        compiler_params=_params("arbitrary"),
        name="retention_states",
    )(p, p, cos, sin, cos, sin, dfl, dbl, dfc, dbc, s0f, s0b)

    gi = lax.broadcasted_iota(jnp.int32, (RET_W, RET_W), 0) // RET_DV
    gj = lax.broadcasted_iota(jnp.int32, (RET_W, RET_W), 1) // RET_DV
    ones_bd = (gi == gj).astype(BF16)
    n_sub = min(nc, 4)
    tm = n_sub * c
    small = lambda shape: pl.BlockSpec(shape, lambda b, i: (0,) * len(shape))
    rope = pl.BlockSpec((tm, RET_QK), lambda b, i: (i, 0))
    chunk_states = pl.BlockSpec((None, n_sub, RET_QK, RET_W), lambda b, i: (b, i, 0, 0))
    y = pl.pallas_call(
        functools.partial(_ret_out_kernel, rotate=rotate, n_sub=n_sub),
        grid=(B, T // tm),
        in_specs=[
            pl.BlockSpec((None, tm, 1024), lambda b, i: (b, i, 0)),
            rope, rope,
            small((1, RET_QK)), small((1, RET_QK)), small((1, RET_HEADS * c)), small((1, RET_HEADS * c)),
            chunk_states, chunk_states,
            small((RET_W, RET_W)),
        ],
        out_specs=pl.BlockSpec((None, tm, RET_W), lambda b, i: (b, i, 0)),
        out_shape=jax.ShapeDtypeStruct((B, T, RET_W), BF16),
        compiler_params=_params("parallel", "parallel"),
        name="retention_outputs",
    )(p, cos, sin, dfl, dbl, dfw, dbw, sf_all, sb_all, ones_bd)
    return y, sf_fin, sb_fin


def _pool_kernel(prev_ref, x_ref, next_ref, w_ref, scale_ref, o_ref, *, tp, t_total):
    i = pl.program_id(1)
    x = x_ref[...]
    prev = jnp.where(i > 0, prev_ref[...], 0.0)
    nxt = jnp.where(i < pl.num_programs(1) - 1, next_ref[...], 0.0)
    xp = jnp.concatenate([prev, x, nxt], axis=0)
    rows = tp + 2 * HALO
    back = lambda a, s: pltpu.roll(a, s, axis=0)
    ahead = lambda a, s: pltpu.roll(a, rows - s, axis=0)
    a2 = xp + back(xp, 1)
    a4 = a2 + back(a2, 2)
    a8 = a4 + back(a4, 4)
    a16 = a8 + back(a8, 8)
    lane = lax.broadcasted_iota(jnp.int32, (rows, POOL_W), 1)
    group = lane // POOL_GROUP
    win = jnp.where(group == 0, a2, jnp.where(group == 1, ahead(a4, 1), jnp.where(group == 2, ahead(a8, 3), ahead(a16, 7))))
    win = win[HALO:HALO + tp]

    lane = lax.broadcasted_iota(jnp.int32, (tp, POOL_W), 1)
    half = jnp.left_shift(1, lane // POOL_GROUP)
    t = lax.broadcasted_iota(jnp.int32, (tp, POOL_W), 0) + i * tp
    count = jnp.minimum(t + half, t_total) - jnp.maximum(t - half, 0)
    d = win / count.astype(F32) - x
    y = jnp.dot(d.astype(BF16), w_ref[...], preferred_element_type=F32) * scale_ref[...]
    o_ref[...] = y.astype(o_ref.dtype)


def _pool(p, w_bd, scale):
    B, T, _ = p.shape
    tp = min(T, 1024)
    col = P_POOL // POOL_W
    hb = tp // HALO
    return pl.pallas_call(
        functools.partial(_pool_kernel, tp=tp, t_total=T),
        grid=(B, T // tp),
        in_specs=[
            pl.BlockSpec((None, HALO, POOL_W), lambda b, i: (b, jnp.maximum(i * hb - 1, 0), col)),
            pl.BlockSpec((None, tp, POOL_W), lambda b, i: (b, i, col)),
            pl.BlockSpec((None, HALO, POOL_W), lambda b, i: (b, jnp.minimum((i + 1) * hb, T // HALO - 1), col)),
            pl.BlockSpec((POOL_W, POOL_W), lambda b, i: (0, 0)),
            pl.BlockSpec((1, POOL_W), lambda b, i: (0, 0)),
        ],
        out_specs=pl.BlockSpec((None, tp, POOL_W), lambda b, i: (b, i, 0)),
        out_shape=jax.ShapeDtypeStruct((B, T, POOL_W), BF16),
        compiler_params=_params("parallel", "parallel"),
        name="pooling",
    )(p, p, p, w_bd, scale.reshape(1, POOL_W))


def _mla_proj_kernel(cq_ref, kr_ref, ckv_ref, gq_ref, gkv_ref, wq_ref, wkv_ref, cos_ref, sin_ref,
                     q_ref, k_ref, v_ref, *, rotate):
    scale = (MLA_NOPE + MLA_ROPE) ** -0.5 * LOG2_E
    cq = _rms(cq_ref[...], MLA_Q_RANK) * gq_ref[...]
    q = jnp.dot(cq.astype(BF16), wq_ref[...], preferred_element_type=F32)
    ckv = _rms(ckv_ref[...], MLA_KV_RANK) * gkv_ref[...]
    kv = jnp.dot(ckv.astype(BF16), wkv_ref[...], preferred_element_type=F32)
    kr = kr_ref[:, 0:HEAD_PAD]
    if rotate:
        kr = kr * cos_ref[...] + kr_ref[:, HEAD_PAD:2 * HEAD_PAD] * sin_ref[...]
    ones_col = (lax.broadcasted_iota(jnp.int32, (1, HEAD_PAD), 1) == MLA_DV).astype(F32)
    left = (pl.program_id(1) % 2 == 0).astype(F32)
    for h in range(MLA_HEADS):
        base = 2 * HEAD_PAD * h
        qh = q[:, base:base + HEAD_PAD]
        if rotate:
            qh = qh * cos_ref[...] + q[:, base + HEAD_PAD:base + 2 * HEAD_PAD] * sin_ref[...]
        q_ref[h] = (qh * scale).astype(BF16)
        k_ref[h] = (kv[:, base:base + HEAD_PAD] + kr).astype(BF16)
        vh = kv[:, base + HEAD_PAD:base + 2 * HEAD_PAD] + ones_col
        v_ref[h] = jnp.concatenate([vh * left, vh * (1.0 - left)], axis=-1).astype(BF16)


def _mla_projection(p, gq, gkv, wq, wkv, cos, sin, rotate):
    B, T, _ = p.shape
    tm = V_ROWS
    heads = pl.BlockSpec((None, MLA_HEADS, tm, HEAD_PAD), lambda b, i: (b, 0, i, 0))
    shape = jax.ShapeDtypeStruct((B, MLA_HEADS, T, HEAD_PAD), BF16)
    v_heads = pl.BlockSpec((None, MLA_HEADS, tm, 2 * HEAD_PAD), lambda b, i: (b, 0, i, 0))
    v_shape = jax.ShapeDtypeStruct((B, MLA_HEADS, T, 2 * HEAD_PAD), BF16)
    const = lambda shape_: _resident(shape_, lambda b, i: (0, 0))
    return pl.pallas_call(
        functools.partial(_mla_proj_kernel, rotate=rotate),
        grid=(B, T // tm),
        in_specs=[
            pl.BlockSpec((None, tm, 512), lambda b, i: (b, i, P_CQ // 512)),
            pl.BlockSpec((None, tm, 256), lambda b, i: (b, i, P_KR // 256)),
            pl.BlockSpec((None, tm, 256), lambda b, i: (b, i, P_CKV // 256)),
            const((1, 512)), const((1, 256)), const((512, 2 * HEAD_PAD * MLA_HEADS)),
            const((256, 2 * HEAD_PAD * MLA_HEADS)),
            pl.BlockSpec((tm, HEAD_PAD), lambda b, i: (i, 0)),
            pl.BlockSpec((tm, HEAD_PAD), lambda b, i: (i, 0)),
        ],
        out_specs=[heads, heads, v_heads],
        out_shape=[shape, shape, v_shape],
        compiler_params=_params("parallel", "parallel"),
        name="mla_projection",
    )(p, p, p, gq, gkv, wq, wkv, cos, sin)


def _attn_kernel(q_ref, kc_ref, vc_ref, *rest, tk, n_lat, heads):
    if n_lat:
        kl_ref, vl_ref, o_ref, m_ref, acc_ref, s_ref = rest
    else:
        o_ref, m_ref, acc_ref = rest
    m_ref[...] = jnp.full_like(m_ref, -1e30)
    acc_ref[...] = jnp.zeros_like(acc_ref)

    def scores(h, k):
        return lax.dot_general(q_ref[h], k, (((1,), (1,)), ((), ())), preferred_element_type=F32)

    def update(h, s, v):
        m_old = m_ref[h]
        m_new = jnp.maximum(m_old[:, 0:1], jnp.max(s, axis=-1, keepdims=True))
        p = jnp.exp2((s - m_new).astype(BF16))
        alpha = jnp.exp2(m_old - m_new)
        pv = jnp.dot(p, v, preferred_element_type=F32)
        acc_ref[h] = jnp.concatenate([alpha, alpha], axis=-1) * acc_ref[h] + pv
        m_ref[h] = jnp.broadcast_to(m_new, m_old.shape)

    def lat(ref, h, j):
        return ref[h, pl.ds(pl.multiple_of(j * tk, tk), tk), :]

    def half(j, cur, nxt):
        for h in range(heads):
            if nxt is not None:
                s_ref[nxt, h] = scores(h, lat(kl_ref, h, j + 1))
            update(h, s_ref[cur, h], lat(vl_ref, h, j))

    ctx_scores = [scores(h, kc_ref[h]) for h in range(heads)]
    if n_lat:
        for h in range(heads):
            s_ref[0, h] = scores(h, lat(kl_ref, h, 0))
    for h in range(heads):
        update(h, ctx_scores[h], vc_ref[h])
    if n_lat:
        def body(jj, carry):
            half(2 * jj, 0, 1)
            half(2 * jj + 1, 1, 0)
            return carry

        lax.fori_loop(0, n_lat // 2 - 1, body, 0)
        half(n_lat - 2, 0, 1)
        half(n_lat - 1, 1, None)
    outs = []
    for h in range(heads):
        acc = acc_ref[h][:, 0:HEAD_PAD] + acc_ref[h][:, HEAD_PAD:2 * HEAD_PAD]
        outs.append(acc[:, 0:MLA_DV] / acc[:, MLA_DV:MLA_DV + 1])
    o_ref[...] = jnp.concatenate(outs, axis=-1).astype(o_ref.dtype)


def _attention(q, k_ctx, v_ctx, k_lat=None, v_lat=None):
    B, H, Tq, _ = q.shape
    Tc = k_ctx.shape[2]
    heads = 2
    tq = min(Tq, 1024)
    tk = 1024
    n_lat = 0 if k_lat is None else k_lat.shape[2] // tk
    assert tk % (2 * V_ROWS) == 0 and Tc % V_ROWS == 0
    whole = lambda t, w: pl.BlockSpec((None, heads, t, w), lambda b, g, i: (b, g, 0, 0))
    in_specs = [pl.BlockSpec((None, heads, tq, HEAD_PAD), lambda b, g, i: (b, g, i, 0)),
                whole(Tc, HEAD_PAD), whole(Tc, 2 * HEAD_PAD)]
    args = [q, k_ctx, v_ctx]
    scratch = [pltpu.VMEM((heads, tq, HEAD_PAD), F32), pltpu.VMEM((heads, tq, 2 * HEAD_PAD), F32)]
    if n_lat:
        assert n_lat % 2 == 0
        in_specs += [whole(k_lat.shape[2], HEAD_PAD), whole(k_lat.shape[2], 2 * HEAD_PAD)]
        args += [k_lat, v_lat]
        scratch.append(pltpu.VMEM((2, heads, tq, tk), F32))
    return pl.pallas_call(
        functools.partial(_attn_kernel, tk=tk, n_lat=n_lat, heads=heads),
        grid=(B, H // heads, Tq // tq),
        in_specs=in_specs,
        out_specs=pl.BlockSpec((None, tq, heads * MLA_DV), lambda b, g, i: (b, i, g)),
        out_shape=jax.ShapeDtypeStruct((B, Tq, H * MLA_DV), BF16),
        scratch_shapes=scratch,
        compiler_params=_params("parallel", "parallel", "parallel"),
        name="attention",
    )(*args)


def _outproj_kernel(x_ref, yr_ref, ym_ref, yp_ref, gate_ref, wr_ref, wm_ref, wp_ref, o_ref):
    y = jnp.dot(yr_ref[...], wr_ref[...], preferred_element_type=F32)
    y = y + jnp.dot(ym_ref[...], wm_ref[...], preferred_element_type=F32)
    y = y + jnp.dot(yp_ref[...], wp_ref[...], preferred_element_type=F32)
    o_ref[...] = x_ref[...] + gate_ref[...] * y


def _out_projection(x, y_ret, y_mla, y_pool, mod, row_of_batch, w_r, w_m, w_p):
    B, T, D = x.shape
    tm = min(T, 512)
    rows = lambda w: pl.BlockSpec((None, tm, w), lambda b, i: (b, i, 0))
    const = lambda shape: _resident(shape, lambda b, i: (0, 0))
    return pl.pallas_call(
        _outproj_kernel,
        grid=(B, T // tm),
        in_specs=[
            rows(D), rows(RET_W), rows(MLA_W), rows(POOL_W),
            pl.BlockSpec((None, 1, D), _mod_index(2, row_of_batch)),
            const((RET_W, D)), const((MLA_W, D)), const((POOL_W, D)),
        ],
        out_specs=rows(D),
        out_shape=jax.ShapeDtypeStruct((B, T, D), F32),
        compiler_params=_params("parallel", "parallel"),
        name="out_projection",
    )(x, y_ret, y_mla, y_pool, mod, w_r, w_m, w_p)


def _ffn_kernel(xp_ref, x_ref, xn_ref, g_ref, sh_ref, sc_ref, gate_ref, wup_ref, cw_ref, cb_ref, wd_ref, fg_ref,
                o_ref, h_ref, ua_ref, ub_ref, act_ref, *, tm, fc, final):
    i = pl.program_id(1)
    d = x_ref.shape[-1]
    d_ff = wd_ref.shape[0]
    n_chunks = d_ff // fc

    def norm_mod(x):
        h = _rms(x, d) * g_ref[...]
        return h * (1.0 + sc_ref[...]) + sh_ref[...]

    h_ref[0:HALO, :] = jnp.where(i > 0, norm_mod(xp_ref[...]), 0.0).astype(BF16)
    h_ref[HALO:HALO + tm, :] = norm_mod(x_ref[...]).astype(BF16)
    h_ref[HALO + tm:2 * HALO + tm, :] = jnp.where(i < pl.num_programs(1) - 1, norm_mod(xn_ref[...]), 0.0).astype(BF16)
    rows = tm + 2 * HALO

    def up(u_ref, col, slot):
        u_ref[slot] = jnp.dot(h_ref[...], wup_ref[:, col:col + fc], preferred_element_type=F32)

    def conv(u_ref, slot, col):
        cw = cw_ref[:, col:col + fc]
        u = u_ref[slot]
        y = pltpu.roll(u, 1, axis=0) * cw[0:1] + u * cw[1:2] + pltpu.roll(u, rows - 1, axis=0) * cw[2:3]
        return y[HALO:HALO + tm] + cb_ref[:, col:col + fc]

    up(ua_ref, 0, 0)
    up(ub_ref, d_ff, 0)
    for f in range(n_chunks):
        cur, more = f % 2, f + 1 < n_chunks
        if more:
            up(ua_ref, (f + 1) * fc, 1 - cur)
        a = _silu(conv(ua_ref, cur, f * fc))
        if more:
            up(ub_ref, d_ff + (f + 1) * fc, 1 - cur)
        b = conv(ub_ref, cur, d_ff + f * fc)
        act_ref[:, f * fc:(f + 1) * fc] = (a * b).astype(BF16)
    acc = jnp.dot(act_ref[...], wd_ref[...], preferred_element_type=F32)
    y = x_ref[...] + gate_ref[...] * acc
    if final:
        y = _rms(y, d) * fg_ref[...]
    o_ref[...] = y


def _ffn(x, g, mod, row_of_batch, w_up, conv_w, conv_b, w_down, final_g, final):
    B, T, D = x.shape
    d_ff = w_down.shape[0]
    fc = 256
    assert d_ff % fc == 0
    tm = min(T, 512)
    hb = tm // HALO
    vec = lambda sec: pl.BlockSpec((None, 1, D), _mod_index(sec, row_of_batch))
    const2 = lambda a: _resident(a.shape, lambda b, i: (0, 0))
    return pl.pallas_call(
        functools.partial(_ffn_kernel, tm=tm, fc=fc, final=final),
        grid=(B, T // tm),
        in_specs=[
            pl.BlockSpec((None, HALO, D), lambda b, i: (b, jnp.maximum(i * hb - 1, 0), 0)),
            pl.BlockSpec((None, tm, D), lambda b, i: (b, i, 0)),
            pl.BlockSpec((None, HALO, D), lambda b, i: (b, jnp.minimum((i + 1) * hb, T // HALO - 1), 0)),
            pl.BlockSpec((1, D), lambda b, i: (0, 0)),
            vec(3), vec(4), vec(5),
            const2(w_up), const2(conv_w), const2(conv_b), const2(w_down),
            pl.BlockSpec((1, D), lambda b, i: (0, 0)),
        ],
        out_specs=pl.BlockSpec((None, tm, D), lambda b, i: (b, i, 0)),
        out_shape=jax.ShapeDtypeStruct((B, T, D), F32),
        scratch_shapes=[pltpu.VMEM((tm + 2 * HALO, D), BF16), pltpu.VMEM((2, tm + 2 * HALO, fc), F32),
                        pltpu.VMEM((2, tm + 2 * HALO, fc), F32), pltpu.VMEM((tm, d_ff), BF16)],
        compiler_params=_params("parallel", "parallel"),
        name="conv_ffn",
    )(x, x, x, g.reshape(1, D), mod, mod, mod, w_up, conv_w, conv_b, w_down, final_g.reshape(1, D))


def _swap_halves(w, groups, width):
    lead = w.shape[:-1]
    return jnp.flip(w.reshape(*lead, groups, 2, width // 2), axis=-2).reshape(*lead, groups * width)


def _layout_in_proj(w_in):
    D = w_in.shape[0]
    offs = [0]
    for s in (RET_QK, RET_QK, RET_W, RET_W, MLA_Q_RANK, MLA_KV_RANK, MLA_ROPE, POOL_W):
        offs.append(offs[-1] + s)
    rq, rk, rv, rg, cq, ckv, kr, pool = [w_in[:, offs[j]:offs[j + 1]] for j in range(8)]
    z = lambda n: jnp.zeros((D, n), w_in.dtype)
    place = lambda a: jnp.concatenate([z(MLA_NOPE), a, z(HEAD_PAD - MLA_NOPE - MLA_ROPE)], axis=1)
    cols = [rk, _swap_halves(rk, RET_HEADS, RET_DK), rv, rq, _swap_halves(rq, RET_HEADS, RET_DK), rg,
            cq, z(512 - MLA_Q_RANK), place(kr), place(_swap_halves(kr, 1, MLA_ROPE)), ckv, pool]
    return jnp.concatenate(cols, axis=1).astype(BF16)


def _layout_uq(w_uq):
    r = w_uq.shape[0]
    w = w_uq.reshape(r, MLA_HEADS, MLA_NOPE + MLA_ROPE)
    nope, rope = w[..., :MLA_NOPE], w[..., MLA_NOPE:]
    z = lambda n: jnp.zeros((r, MLA_HEADS, n), w_uq.dtype)
    tail = HEAD_PAD - MLA_NOPE - MLA_ROPE
    per_head = jnp.concatenate([nope, rope, z(tail), z(MLA_NOPE), _swap_halves(rope, 1, MLA_ROPE), z(tail)], axis=-1)
    w = per_head.reshape(r, MLA_HEADS * 2 * HEAD_PAD)
    return jnp.concatenate([w, jnp.zeros((512 - r, w.shape[1]), w.dtype)], axis=0).astype(BF16)


def _layout_ukv(w_ukv):
    r = w_ukv.shape[0]
    w = w_ukv.reshape(r, MLA_HEADS, MLA_NOPE + MLA_DV)
    z = lambda n: jnp.zeros((r, MLA_HEADS, n), w_ukv.dtype)
    per_head = jnp.concatenate([w[..., :MLA_NOPE], z(HEAD_PAD - MLA_NOPE), w[..., MLA_NOPE:], z(HEAD_PAD - MLA_DV)], axis=-1)
    return per_head.reshape(r, MLA_HEADS * 2 * HEAD_PAD).astype(BF16)


def _rope_tables(t):
    rows = t // GRID_W
    row = jnp.repeat(jnp.arange(rows, dtype=F32), GRID_W)
    col = jnp.tile(jnp.arange(GRID_W, dtype=F32), rows)
    n_freq = MLA_ROPE // 4
    inv = ROPE_THETA ** (-jnp.arange(n_freq, dtype=F32) / n_freq)
    ang = jnp.concatenate([row[:, None] * inv, col[:, None] * inv], axis=-1)
    cos, sin = jnp.cos(ang), jnp.sin(ang)
    cc = jnp.concatenate([cos, cos], axis=-1)
    ss = jnp.concatenate([-sin, sin], axis=-1)
    ret_cos, ret_sin = jnp.tile(cc, (1, RET_HEADS)), jnp.tile(ss, (1, RET_HEADS))
    tail = HEAD_PAD - MLA_NOPE - MLA_ROPE
    one, zero = jnp.ones((t, MLA_NOPE), F32), jnp.zeros((t, MLA_NOPE), F32)
    mla_cos = jnp.concatenate([one, cc, jnp.ones((t, tail), F32)], axis=-1)
    mla_sin = jnp.concatenate([zero, ss, jnp.zeros((t, tail), F32)], axis=-1)
    return ret_cos, ret_sin, mla_cos, mla_sin


def kernel(x, c, ctx, c_ctx, w_mod, b_mod, norm1_g, w_in, ret_decay_f, ret_decay_b, mla_q_norm_g, w_uq,
           mla_kv_norm_g, w_ukv, pool_w, pool_scale, w_out, norm2_g, w_up, conv_w, conv_b, w_down,
           final_norm_g):
    B, T, D = x.shape
    L = w_mod.shape[0]
    d_ff = w_down.shape[1]
    assert T % 512 == 0 and ctx.shape[1] % RET_CHUNK == 0 and B + 1 <= SUBLANES

    ret_cos, ret_sin, mla_cos, mla_sin = _rope_tables(T)
    c_rows = jnp.concatenate([c, c_ctx[None, :], jnp.zeros((SUBLANES - B - 1, D), c.dtype)], axis=0)
    mod_all = _modulation(c_rows, w_mod, b_mod)
    lat_row = lambda b: b
    ctx_row = lambda b: B

    xc = ctx
    zero_state = jnp.zeros((B, RET_QK, RET_W), F32)
    for l in range(L):
        with_ctx_out = l < L - 1
        mod = mod_all[l].reshape(SUBLANES * 6, 1, D)
        w_p = _layout_in_proj(w_in[l])
        wq = _layout_uq(w_uq[l])
        wkv = _layout_ukv(w_ukv[l])
        gq = jnp.concatenate([mla_q_norm_g[l], jnp.zeros((512 - MLA_Q_RANK,), F32)]).reshape(1, 512)
        gkv = mla_kv_norm_g[l].reshape(1, MLA_KV_RANK)
        pool_bd = jax.scipy.linalg.block_diag(*[pool_w[l, g] for g in range(len(POOL_WINDOWS))]).astype(BF16)
        wo = w_out[l].astype(BF16)
        wo_r, wo_m, wo_p = wo[:RET_W], wo[RET_W:RET_W + MLA_W], wo[RET_W + MLA_W:]
        ffn_w = (w_up[l].astype(BF16), conv_w[l], conv_b[l].reshape(1, 2 * d_ff), w_down[l].astype(BF16))

        p_lat = _in_projection(x, norm1_g[l], mod, lat_row, w_p)
        p_ctx = _in_projection(xc, norm1_g[l], mod, ctx_row, w_p)

        yr_ctx, s_f, s_b = _retention(p_ctx, ret_cos, ret_sin, ret_decay_f[l], ret_decay_b[l], zero_state, zero_state, False)
        yr_lat, _, _ = _retention(p_lat, ret_cos, ret_sin, ret_decay_f[l], ret_decay_b[l], s_f, s_b, True)

        q_ctx, k_ctx, v_ctx = _mla_projection(p_ctx, gq, gkv, wq, wkv, mla_cos, mla_sin, False)
        q_lat, k_lat, v_lat = _mla_projection(p_lat, gq, gkv, wq, wkv, mla_cos, mla_sin, True)
        ym_lat = _attention(q_lat, k_ctx, v_ctx, k_lat, v_lat)

        yp_lat = _pool(p_lat, pool_bd, pool_scale[l])
        x = _out_projection(x, yr_lat, ym_lat, yp_lat, mod, lat_row, wo_r, wo_m, wo_p)
        x = _ffn(x, norm2_g[l], mod, lat_row, *ffn_w, final_norm_g, not with_ctx_out)
        if with_ctx_out:
            ym_ctx = _attention(q_ctx, k_ctx, v_ctx)
            yp_ctx = _pool(p_ctx, pool_bd, pool_scale[l])
            xc = _out_projection(xc, yr_ctx, ym_ctx, yp_ctx, mod, ctx_row, wo_r, wo_m, wo_p)
            xc = _ffn(xc, norm2_g[l], mod, ctx_row, *ffn_w, final_norm_g, False)
    return x
```

```python
import functools

import jax
import jax.numpy as jnp
from jax import lax
from jax.experimental import pallas as pl
from jax.experimental.pallas import tpu as pltpu

BF16 = jnp.bfloat16
F32 = jnp.float32

GRID_W = 64
ROPE_THETA = 10000.0
NORM_EPS = 1e-6
RET_HEADS = 4
RET_DK = 32
RET_DV = 64
MLA_HEADS = 8
MLA_Q_RANK = 384
MLA_KV_RANK = 256
MLA_NOPE = 64
MLA_ROPE = 32
MLA_DV = 64
POOL_WINDOWS = (2, 4, 8, 16)
POOL_GROUP = 64

LANES = 128
SUBLANES = 8
BF16_ROWS = 16
VMEM_LIMIT_BYTES = 56 * 1024 * 1024

RET_W = RET_HEADS * RET_DV
RET_QK = RET_HEADS * RET_DK
MLA_W = MLA_HEADS * MLA_DV
POOL_W = len(POOL_WINDOWS) * POOL_GROUP
HEAD_PAD = 128
HALO = 16
LOG2_E = 1.4426950408889634
V_ROWS = 256
RET_CHUNK = 128

P_RET = 0
P_CQ = 1024
P_KR = 1536
P_CKV = 1792
P_POOL = 2048
P_WIDTH = 2304


def _params(*semantics):
    return pltpu.CompilerParams(dimension_semantics=semantics, vmem_limit_bytes=VMEM_LIMIT_BYTES)


def _resident(block_shape, index_map):
    return pl.BlockSpec(block_shape, index_map, pipeline_mode=pl.Buffered(1))


def _rms(x, width):
    return x * lax.rsqrt(jnp.sum(x * x, axis=-1, keepdims=True) * (1.0 / width) + NORM_EPS)


def _silu(x):
    return x * jax.nn.sigmoid(x)


def _log_sigmoid(d):
    return jnp.minimum(d, 0.0) - jnp.log1p(jnp.exp(-jnp.abs(d)))


def _mod_kernel(c_ref, w_ref, b_ref, o_ref):
    s = _silu(c_ref[...])
    o_ref[...] = jnp.dot(s.astype(BF16), w_ref[...].astype(BF16), preferred_element_type=F32) + b_ref[...]


def _modulation(c_rows, w_mod, b_mod):
    L, D, W = w_mod.shape
    tn = 1536
    return pl.pallas_call(
        _mod_kernel,
        grid=(L, W // tn),
        in_specs=[
            pl.BlockSpec((SUBLANES, D), lambda l, j: (0, 0)),
            pl.BlockSpec((None, D, tn), lambda l, j: (l, 0, j)),
            pl.BlockSpec((None, 1, tn), lambda l, j: (l, 0, j)),
        ],
        out_specs=pl.BlockSpec((None, SUBLANES, tn), lambda l, j: (l, 0, j)),
        out_shape=jax.ShapeDtypeStruct((L, SUBLANES, W), F32),
        compiler_params=_params("parallel", "parallel"),
        name="modulation",
    )(c_rows, w_mod, b_mod.reshape(L, 1, W))


def _inproj_kernel(x_ref, g_ref, sh_ref, sc_ref, w_ref, o_ref):
    x = x_ref[...]
    h = _rms(x, x.shape[-1]) * g_ref[...]
    h = h * (1.0 + sc_ref[...]) + sh_ref[...]
    o_ref[...] = jnp.dot(h.astype(BF16), w_ref[...], preferred_element_type=F32)


def _mod_index(section, row_of_batch):
    def index(b, *_):
        return (row_of_batch(b) * 6 + section, 0, 0)

    return index


def _in_projection(x, g, mod, row_of_batch, w):
    B, T, D = x.shape
    tm = min(T, 512)
    vec = lambda sec: pl.BlockSpec((None, 1, D), _mod_index(sec, row_of_batch))
    return pl.pallas_call(
        _inproj_kernel,
        grid=(B, T // tm),
        in_specs=[
            pl.BlockSpec((None, tm, D), lambda b, i: (b, i, 0)),
            pl.BlockSpec((1, D), lambda b, i: (0, 0)),
            vec(0),
            vec(1),
            _resident((D, P_WIDTH), lambda b, i: (0, 0)),
        ],
        out_specs=pl.BlockSpec((None, tm, P_WIDTH), lambda b, i: (b, i, 0)),
        out_shape=jax.ShapeDtypeStruct((B, T, P_WIDTH), F32),
        compiler_params=_params("parallel", "parallel"),
        name="in_projection",
    )(x, g.reshape(1, D), mod, mod, w)


def _ret_rope(a, a_swap, cos, sin, rotate):
    if not rotate:
        return a
    return a * cos + a_swap * sin


def _ret_state_kernel(pf_ref, pb_ref, cf_ref, sf_ref, cb_ref, sb_ref, dfl_ref, dbl_ref, dfc_ref, dbc_ref,
                      s0f_ref, s0b_ref, sf_all_ref, sb_all_ref, sf_fin_ref, sb_fin_ref, stf_ref, stb_ref,
                      *, rotate):
    c = RET_CHUNK
    i = pl.program_id(0)

    @pl.when(i == 0)
    def _():
        stf_ref[...] = s0f_ref[...]
        stb_ref[...] = s0b_ref[...]

    sf_all_ref[...] = stf_ref[...]
    sb_all_ref[...] = stb_ref[...]

    n = lax.broadcasted_iota(jnp.int32, (c, RET_QK), 0).astype(F32)
    rows = lax.broadcasted_iota(jnp.int32, (RET_QK, RET_W), 0)
    cols = lax.broadcasted_iota(jnp.int32, (RET_QK, RET_W), 1)
    head_block = (rows // RET_DK) == (cols // RET_DV)

    def update(p_ref, cos_ref, sin_ref, lg_lane, lg_col, pos_pow, st_ref):
        k_decay = jnp.exp(lg_lane * pos_pow) * (RET_DK ** -0.5)
        carry_decay = jnp.exp(lg_col * float(c))
        for b in range(p_ref.shape[0]):
            k = _ret_rope(p_ref[b, :, 0:RET_QK], p_ref[b, :, RET_QK:2 * RET_QK], cos_ref[...], sin_ref[...], rotate)
            v = p_ref[b, :, 2 * RET_QK:2 * RET_QK + RET_W]
            kv = lax.dot_general((k * k_decay).astype(BF16), v.astype(BF16), (((0,), (0,)), ((), ())),
                                 preferred_element_type=F32)
            st_ref[b] = st_ref[b] * carry_decay + jnp.where(head_block, kv, 0.0)

    update(pf_ref, cf_ref, sf_ref, _log_sigmoid(dfl_ref[...]), _log_sigmoid(dfc_ref[...]), (c - 1.0) - n, stf_ref)
    update(pb_ref, cb_ref, sb_ref, _log_sigmoid(dbl_ref[...]), _log_sigmoid(dbc_ref[...]), n, stb_ref)

    @pl.when(i == pl.num_programs(0) - 1)
    def _():
        sf_fin_ref[...] = stf_ref[...]
        sb_fin_ref[...] = stb_ref[...]


def _ret_out_kernel(p_ref, cos_ref, sin_ref, dfl_ref, dbl_ref, dfw_ref, dbw_ref, sf_ref, sb_ref, ones_ref,
                    o_ref, *, rotate, n_sub):
    c = RET_CHUNK
    kr = lax.broadcasted_iota(jnp.int32, (RET_HEADS * c, RET_QK), 0) // c
    kl = lax.broadcasted_iota(jnp.int32, (RET_HEADS * c, RET_QK), 1) // RET_DK
    vr = lax.broadcasted_iota(jnp.int32, (RET_HEADS * c, RET_W), 0) // c
    vl = lax.broadcasted_iota(jnp.int32, (RET_HEADS * c, RET_W), 1) // RET_DV

    lgf_w = _log_sigmoid(dfw_ref[...])
    lgb_w = _log_sigmoid(dbw_ref[...])
    nq = lax.broadcasted_iota(jnp.int32, (c, RET_HEADS * c), 0)
    mk = lax.broadcasted_iota(jnp.int32, (c, RET_HEADS * c), 1) % c
    diff = (nq - mk).astype(F32)
    dmat = jnp.exp(jnp.where(diff >= 0.0, lgf_w, -lgb_w) * diff)
    n = lax.broadcasted_iota(jnp.int32, (c, RET_QK), 0).astype(F32)
    qf_decay = jnp.exp(_log_sigmoid(dfl_ref[...]) * (n + 1.0))
    qb_decay = jnp.exp(_log_sigmoid(dbl_ref[...]) * (float(c) - n))

    for s in range(n_sub):
        r = slice(s * c, (s + 1) * c)
        cos, sin = cos_ref[r, :], sin_ref[r, :]
        k = _ret_rope(p_ref[r, 0:RET_QK], p_ref[r, RET_QK:2 * RET_QK], cos, sin, rotate) * (RET_DK ** -0.5)
        v = p_ref[r, 2 * RET_QK:2 * RET_QK + RET_W]
        q = _ret_rope(p_ref[r, 512:512 + RET_QK], p_ref[r, 512 + RET_QK:512 + 2 * RET_QK], cos, sin, rotate)
        gate = p_ref[r, 768:768 + RET_W]
        k_stack = jnp.where(kr == kl, jnp.tile(k, (RET_HEADS, 1)), 0.0).astype(BF16)
        v_stack = jnp.where(vr == vl, jnp.tile(v, (RET_HEADS, 1)), 0.0).astype(BF16)

        scores = lax.dot_general(q.astype(BF16), k_stack, (((1,), (1,)), ((), ())), preferred_element_type=F32)
        a = (scores * dmat).astype(BF16)
        o = jnp.dot(a, v_stack, preferred_element_type=F32)
        o = o + jnp.dot((q * qf_decay).astype(BF16), sf_ref[s].astype(BF16), preferred_element_type=F32)
        o = o + jnp.dot((q * qb_decay).astype(BF16), sb_ref[s].astype(BF16), preferred_element_type=F32)

        sq = o * o
        hi = sq.astype(BF16)
        lo = (sq - hi.astype(F32)).astype(BF16)
        ssum = (jnp.dot(hi, ones_ref[...], preferred_element_type=F32)
                + jnp.dot(lo, ones_ref[...], preferred_element_type=F32))
        o = o * lax.rsqrt(ssum * (1.0 / RET_DV) + NORM_EPS)
        o_ref[r, :] = (o * _silu(gate)).astype(o_ref.dtype)


def _retention(p, cos, sin, decay_f, decay_b, s0f, s0b, rotate):
    B, T, _ = p.shape
    c = RET_CHUNK
    nc = T // c
    rep = lambda d, r: jnp.repeat(d.astype(F32), r)
    dfl, dbl = rep(decay_f, RET_DK).reshape(1, RET_QK), rep(decay_b, RET_DK).reshape(1, RET_QK)
    dfc, dbc = dfl.reshape(RET_QK, 1), dbl.reshape(RET_QK, 1)
    dfw, dbw = rep(decay_f, c).reshape(1, RET_HEADS * c), rep(decay_b, c).reshape(1, RET_HEADS * c)
    small1 = lambda shape: pl.BlockSpec(shape, lambda i: (0,) * len(shape))
    rope_f = pl.BlockSpec((c, RET_QK), lambda i: (i, 0))
    rope_b = pl.BlockSpec((c, RET_QK), lambda i: (nc - 1 - i, 0))
    state = pl.BlockSpec((B, RET_QK, RET_W), lambda i: (0, 0, 0))

    sf_all, sb_all, sf_fin, sb_fin = pl.pallas_call(
        functools.partial(_ret_state_kernel, rotate=rotate),
        grid=(nc,),
        in_specs=[
            pl.BlockSpec((B, c, 512), lambda i: (0, i, 0)),
            pl.BlockSpec((B, c, 512), lambda i: (0, nc - 1 - i, 0)),
            rope_f, rope_f, rope_b, rope_b,
            small1((1, RET_QK)), small1((1, RET_QK)), small1((RET_QK, 1)), small1((RET_QK, 1)),
            state, state,
        ],
        out_specs=[
            pl.BlockSpec((B, None, RET_QK, RET_W), lambda i: (0, i, 0, 0)),
            pl.BlockSpec((B, None, RET_QK, RET_W), lambda i: (0, nc - 1 - i, 0, 0)),
            state, state,
        ],
        out_shape=[
            jax.ShapeDtypeStruct((B, nc, RET_QK, RET_W), F32),
            jax.ShapeDtypeStruct((B, nc, RET_QK, RET_W), F32),
            jax.ShapeDtypeStruct((B, RET_QK, RET_W), F32),
            jax.ShapeDtypeStruct((B, RET_QK, RET_W), F32),
        ],
        scratch_shapes=[pltpu.VMEM((B, RET_QK, RET_W), F32), pltpu.VMEM((B, RET_QK, RET_W), F32)],
        compiler_params=_params("arbitrary"),
        name="retention_states",
    )(p, p, cos, sin, cos, sin, dfl, dbl, dfc, dbc, s0f, s0b)

    gi = lax.broadcasted_iota(jnp.int32, (RET_W, RET_W), 0) // RET_DV
    gj = lax.broadcasted_iota(jnp.int32, (RET_W, RET_W), 1) // RET_DV
    ones_bd = (gi == gj).astype(BF16)
    n_sub = min(nc, 4)
    tm = n_sub * c
    small = lambda shape: pl.BlockSpec(shape, lambda b, i: (0,) * len(shape))
    rope = pl.BlockSpec((tm, RET_QK), lambda b, i: (i, 0))
    chunk_states = pl.BlockSpec((None, n_sub, RET_QK, RET_W), lambda b, i: (b, i, 0, 0))
    y = pl.pallas_call(
        functools.partial(_ret_out_kernel, rotate=rotate, n_sub=n_sub),
        grid=(B, T // tm),
        in_specs=[
            pl.BlockSpec((None, tm, 1024), lambda b, i: (b, i, 0)),
            rope, rope,
            small((1, RET_QK)), small((1, RET_QK)), small((1, RET_HEADS * c)), small((1, RET_HEADS * c)),
            chunk_states, chunk_states,
            small((RET_W, RET_W)),
        ],
        out_specs=pl.BlockSpec((None, tm, RET_W), lambda b, i: (b, i, 0)),
        out_shape=jax.ShapeDtypeStruct((B, T, RET_W), BF16),
        compiler_params=_params("parallel", "parallel"),
        name="retention_outputs",
    )(p, cos, sin, dfl, dbl, dfw, dbw, sf_all, sb_all, ones_bd)
    return y, sf_fin, sb_fin


def _pool_kernel(prev_ref, x_ref, next_ref, w_ref, scale_ref, o_ref, *, tp, t_total):
    i = pl.program_id(1)
    x = x_ref[...]
    prev = jnp.where(i > 0, prev_ref[...], 0.0)
    nxt = jnp.where(i < pl.num_programs(1) - 1, next_ref[...], 0.0)
    xp = jnp.concatenate([prev, x, nxt], axis=0)
    rows = tp + 2 * HALO
    back = lambda a, s: pltpu.roll(a, s, axis=0)
    ahead = lambda a, s: pltpu.roll(a, rows - s, axis=0)
    a2 = xp + back(xp, 1)
    a4 = a2 + back(a2, 2)
    a8 = a4 + back(a4, 4)
    a16 = a8 + back(a8, 8)
    lane = lax.broadcasted_iota(jnp.int32, (rows, POOL_W), 1)
    group = lane // POOL_GROUP
    win = jnp.where(group == 0, a2, jnp.where(group == 1, ahead(a4, 1), jnp.where(group == 2, ahead(a8, 3), ahead(a16, 7))))
    win = win[HALO:HALO + tp]

    lane = lax.broadcasted_iota(jnp.int32, (tp, POOL_W), 1)
    half = jnp.left_shift(1, lane // POOL_GROUP)
    t = lax.broadcasted_iota(jnp.int32, (tp, POOL_W), 0) + i * tp
    count = jnp.minimum(t + half, t_total) - jnp.maximum(t - half, 0)
    d = win / count.astype(F32) - x
    y = jnp.dot(d.astype(BF16), w_ref[...], preferred_element_type=F32) * scale_ref[...]
    o_ref[...] = y.astype(o_ref.dtype)


def _pool(p, w_bd, scale):
    B, T, _ = p.shape
    tp = min(T, 1024)
    col = P_POOL // POOL_W
    hb = tp // HALO
    return pl.pallas_call(
        functools.partial(_pool_kernel, tp=tp, t_total=T),
        grid=(B, T // tp),
        in_specs=[
            pl.BlockSpec((None, HALO, POOL_W), lambda b, i: (b, jnp.maximum(i * hb - 1, 0), col)),
            pl.BlockSpec((None, tp, POOL_W), lambda b, i: (b, i, col)),
            pl.BlockSpec((None, HALO, POOL_W), lambda b, i: (b, jnp.minimum((i + 1) * hb, T // HALO - 1), col)),
            pl.BlockSpec((POOL_W, POOL_W), lambda b, i: (0, 0)),
            pl.BlockSpec((1, POOL_W), lambda b, i: (0, 0)),
        ],
        out_specs=pl.BlockSpec((None, tp, POOL_W), lambda b, i: (b, i, 0)),
        out_shape=jax.ShapeDtypeStruct((B, T, POOL_W), BF16),
        compiler_params=_params("parallel", "parallel"),
        name="pooling",
    )(p, p, p, w_bd, scale.reshape(1, POOL_W))


def _mla_proj_kernel(cq_ref, kr_ref, ckv_ref, gq_ref, gkv_ref, wq_ref, wkv_ref, cos_ref, sin_ref,
                     q_ref, k_ref, v_ref, *, rotate):
    scale = (MLA_NOPE + MLA_ROPE) ** -0.5 * LOG2_E
    cq = _rms(cq_ref[...], MLA_Q_RANK) * gq_ref[...]
    q = jnp.dot(cq.astype(BF16), wq_ref[...], preferred_element_type=F32)
    ckv = _rms(ckv_ref[...], MLA_KV_RANK) * gkv_ref[...]
    kv = jnp.dot(ckv.astype(BF16), wkv_ref[...], preferred_element_type=F32)
    kr = kr_ref[:, 0:HEAD_PAD]
    if rotate:
        kr = kr * cos_ref[...] + kr_ref[:, HEAD_PAD:2 * HEAD_PAD] * sin_ref[...]
    ones_col = (lax.broadcasted_iota(jnp.int32, (1, HEAD_PAD), 1) == MLA_DV).astype(F32)
    zeros = jnp.zeros((V_ROWS, HEAD_PAD), BF16)
    for h in range(MLA_HEADS):
        base = 2 * HEAD_PAD * h
        qh = q[:, base:base + HEAD_PAD]
        if rotate:
            qh = qh * cos_ref[...] + q[:, base + HEAD_PAD:base + 2 * HEAD_PAD] * sin_ref[...]
        q_ref[h] = (qh * scale).astype(BF16)
        k_ref[h] = (kv[:, base:base + HEAD_PAD] + kr).astype(BF16)
        vh = (kv[:, base + HEAD_PAD:base + 2 * HEAD_PAD] + ones_col).astype(BF16)
        for r in range(vh.shape[0] // V_ROWS):
            rows = slice(r * V_ROWS, (r + 1) * V_ROWS)
            halves = [vh[rows], zeros] if r % 2 == 0 else [zeros, vh[rows]]
            v_ref[h, rows, :] = jnp.concatenate(halves, axis=-1)


def _mla_projection(p, gq, gkv, wq, wkv, cos, sin, rotate):
    B, T, _ = p.shape
    tm = min(T, 2 * V_ROWS)
    heads = pl.BlockSpec((None, MLA_HEADS, tm, HEAD_PAD), lambda b, i: (b, 0, i, 0))
    shape = jax.ShapeDtypeStruct((B, MLA_HEADS, T, HEAD_PAD), BF16)
    v_heads = pl.BlockSpec((None, MLA_HEADS, tm, 2 * HEAD_PAD), lambda b, i: (b, 0, i, 0))
    v_shape = jax.ShapeDtypeStruct((B, MLA_HEADS, T, 2 * HEAD_PAD), BF16)
    const = lambda shape_: _resident(shape_, lambda b, i: (0, 0))
    return pl.pallas_call(
        functools.partial(_mla_proj_kernel, rotate=rotate),
        grid=(B, T // tm),
        in_specs=[
            pl.BlockSpec((None, tm, 512), lambda b, i: (b, i, P_CQ // 512)),
            pl.BlockSpec((None, tm, 256), lambda b, i: (b, i, P_KR // 256)),
            pl.BlockSpec((None, tm, 256), lambda b, i: (b, i, P_CKV // 256)),
            const((1, 512)), const((1, 256)), const((512, 2 * HEAD_PAD * MLA_HEADS)),
            const((256, 2 * HEAD_PAD * MLA_HEADS)),
            pl.BlockSpec((tm, HEAD_PAD), lambda b, i: (i, 0)),
            pl.BlockSpec((tm, HEAD_PAD), lambda b, i: (i, 0)),
        ],
        out_specs=[heads, heads, v_heads],
        out_shape=[shape, shape, v_shape],
        compiler_params=_params("parallel", "parallel"),
        name="mla_projection",
    )(p, p, p, gq, gkv, wq, wkv, cos, sin)


def _attn_kernel(q_ref, kc_ref, vc_ref, *rest, tk, n_lat, heads):
    if n_lat:
        kl_ref, vl_ref, o_ref, m_ref, acc_ref, s_ref = rest
    else:
        o_ref, m_ref, acc_ref = rest
    m_ref[...] = jnp.full_like(m_ref, -1e30)
    acc_ref[...] = jnp.zeros_like(acc_ref)

    def scores(h, k):
        return lax.dot_general(q_ref[h], k, (((1,), (1,)), ((), ())), preferred_element_type=F32)

    def update(h, s, v):
        m_old = m_ref[h]
        m_new = jnp.maximum(m_old[:, 0:1], jnp.max(s, axis=-1, keepdims=True))
        p = jnp.exp2(s - m_new)
        alpha = jnp.exp2(m_old - m_new)
        pv = jnp.dot(p.astype(BF16), v, preferred_element_type=F32)
        acc_ref[h] = jnp.concatenate([alpha, alpha], axis=-1) * acc_ref[h] + pv
        m_ref[h] = jnp.broadcast_to(m_new, m_old.shape)

    def lat(ref, h, j):
        return ref[h, pl.ds(pl.multiple_of(j * tk, tk), tk), :]

    def half(j, cur, nxt):
        for h in range(heads):
            if nxt is not None:
                s_ref[nxt, h] = scores(h, lat(kl_ref, h, j + 1))
            update(h, s_ref[cur, h], lat(vl_ref, h, j))

    ctx_scores = [scores(h, kc_ref[h]) for h in range(heads)]
    if n_lat:
        for h in range(heads):
            s_ref[0, h] = scores(h, lat(kl_ref, h, 0))
    for h in range(heads):
        update(h, ctx_scores[h], vc_ref[h])
    if n_lat:
        def body(jj, carry):
            half(2 * jj, 0, 1)
            half(2 * jj + 1, 1, 0)
            return carry

        lax.fori_loop(0, n_lat // 2 - 1, body, 0)
        half(n_lat - 2, 0, 1)
        half(n_lat - 1, 1, None)
    outs = []
    for h in range(heads):
        acc = acc_ref[h][:, 0:HEAD_PAD] + acc_ref[h][:, HEAD_PAD:2 * HEAD_PAD]
        outs.append(acc[:, 0:MLA_DV] / acc[:, MLA_DV:MLA_DV + 1])
    o_ref[...] = jnp.concatenate(outs, axis=-1).astype(o_ref.dtype)


def _attention(q, k_ctx, v_ctx, k_lat=None, v_lat=None):
    B, H, Tq, _ = q.shape
    Tc = k_ctx.shape[2]
    heads = 2
    tq = min(Tq, 1024)
    tk = 1024
    n_lat = 0 if k_lat is None else k_lat.shape[2] // tk
    assert tk % (2 * V_ROWS) == 0 and Tc % V_ROWS == 0
    whole = lambda t, w: pl.BlockSpec((None, heads, t, w), lambda b, g, i: (b, g, 0, 0))
    in_specs = [pl.BlockSpec((None, heads, tq, HEAD_PAD), lambda b, g, i: (b, g, i, 0)),
                whole(Tc, HEAD_PAD), whole(Tc, 2 * HEAD_PAD)]
    args = [q, k_ctx, v_ctx]
    scratch = [pltpu.VMEM((heads, tq, HEAD_PAD), F32), pltpu.VMEM((heads, tq, 2 * HEAD_PAD), F32)]
    if n_lat:
        assert n_lat % 2 == 0
        in_specs += [whole(k_lat.shape[2], HEAD_PAD), whole(k_lat.shape[2], 2 * HEAD_PAD)]
        args += [k_lat, v_lat]
        scratch.append(pltpu.VMEM((2, heads, tq, tk), F32))
    return pl.pallas_call(
        functools.partial(_attn_kernel, tk=tk, n_lat=n_lat, heads=heads),
        grid=(B, H // heads, Tq // tq),
        in_specs=in_specs,
        out_specs=pl.BlockSpec((None, tq, heads * MLA_DV), lambda b, g, i: (b, i, g)),
        out_shape=jax.ShapeDtypeStruct((B, Tq, H * MLA_DV), BF16),
        scratch_shapes=scratch,
        compiler_params=_params("parallel", "parallel", "parallel"),
        name="attention",
    )(*args)


def _outproj_kernel(x_ref, yr_ref, ym_ref, yp_ref, gate_ref, wr_ref, wm_ref, wp_ref, o_ref):
    y = jnp.dot(yr_ref[...], wr_ref[...], preferred_element_type=F32)
    y = y + jnp.dot(ym_ref[...], wm_ref[...], preferred_element_type=F32)
    y = y + jnp.dot(yp_ref[...], wp_ref[...], preferred_element_type=F32)
    o_ref[...] = x_ref[...] + gate_ref[...] * y


def _out_projection(x, y_ret, y_mla, y_pool, mod, row_of_batch, w_r, w_m, w_p):
    B, T, D = x.shape
    tm = min(T, 512)
    rows = lambda w: pl.BlockSpec((None, tm, w), lambda b, i: (b, i, 0))
    const = lambda shape: _resident(shape, lambda b, i: (0, 0))
    return pl.pallas_call(
        _outproj_kernel,
        grid=(B, T // tm),
        in_specs=[
            rows(D), rows(RET_W), rows(MLA_W), rows(POOL_W),
            pl.BlockSpec((None, 1, D), _mod_index(2, row_of_batch)),
            const((RET_W, D)), const((MLA_W, D)), const((POOL_W, D)),
        ],
        out_specs=rows(D),
        out_shape=jax.ShapeDtypeStruct((B, T, D), F32),
        compiler_params=_params("parallel", "parallel"),
        name="out_projection",
    )(x, y_ret, y_mla, y_pool, mod, w_r, w_m, w_p)


def _ffn_kernel(xp_ref, x_ref, xn_ref, g_ref, sh_ref, sc_ref, gate_ref, wup_ref, cw_ref, cb_ref, wd_ref, fg_ref,
                o_ref, h_ref, ua_ref, ub_ref, act_ref, *, tm, fc, final):
    i = pl.program_id(1)
    d = x_ref.shape[-1]
    d_ff = wd_ref.shape[0]
    n_chunks = d_ff // fc

    def norm_mod(x):
        h = _rms(x, d) * g_ref[...]
        return h * (1.0 + sc_ref[...]) + sh_ref[...]

    h_ref[0:HALO, :] = jnp.where(i > 0, norm_mod(xp_ref[...]), 0.0).astype(BF16)
    h_ref[HALO:HALO + tm, :] = norm_mod(x_ref[...]).astype(BF16)
    h_ref[HALO + tm:2 * HALO + tm, :] = jnp.where(i < pl.num_programs(1) - 1, norm_mod(xn_ref[...]), 0.0).astype(BF16)
    rows = tm + 2 * HALO

    def up(u_ref, col, slot):
        u_ref[slot] = jnp.dot(h_ref[...], wup_ref[:, col:col + fc], preferred_element_type=F32)

    def conv(u_ref, slot, col):
        cw = cw_ref[:, col:col + fc]
        u = u_ref[slot]
        y = pltpu.roll(u, 1, axis=0) * cw[0:1] + u * cw[1:2] + pltpu.roll(u, rows - 1, axis=0) * cw[2:3]
        return y[HALO:HALO + tm] + cb_ref[:, col:col + fc]

    up(ua_ref, 0, 0)
    up(ub_ref, d_ff, 0)
    for f in range(n_chunks):
        cur, more = f % 2, f + 1 < n_chunks
        if more:
            up(ua_ref, (f + 1) * fc, 1 - cur)
        a = _silu(conv(ua_ref, cur, f * fc))
        if more:
            up(ub_ref, d_ff + (f + 1) * fc, 1 - cur)
        b = conv(ub_ref, cur, d_ff + f * fc)
        act_ref[:, f * fc:(f + 1) * fc] = (a * b).astype(BF16)
    acc = jnp.dot(act_ref[...], wd_ref[...], preferred_element_type=F32)
    y = x_ref[...] + gate_ref[...] * acc
    if final:
        y = _rms(y, d) * fg_ref[...]
    o_ref[...] = y


def _ffn(x, g, mod, row_of_batch, w_up, conv_w, conv_b, w_down, final_g, final):
    B, T, D = x.shape
    d_ff = w_down.shape[0]
    fc = 256
    assert d_ff % fc == 0
    tm = min(T, 512)
    hb = tm // HALO
    vec = lambda sec: pl.BlockSpec((None, 1, D), _mod_index(sec, row_of_batch))
    const2 = lambda a: _resident(a.shape, lambda b, i: (0, 0))
    return pl.pallas_call(
        functools.partial(_ffn_kernel, tm=tm, fc=fc, final=final),
        grid=(B, T // tm),
        in_specs=[
            pl.BlockSpec((None, HALO, D), lambda b, i: (b, jnp.maximum(i * hb - 1, 0), 0)),
            pl.BlockSpec((None, tm, D), lambda b, i: (b, i, 0)),
            pl.BlockSpec((None, HALO, D), lambda b, i: (b, jnp.minimum((i + 1) * hb, T // HALO - 1), 0)),
            pl.BlockSpec((1, D), lambda b, i: (0, 0)),
            vec(3), vec(4), vec(5),
            const2(w_up), const2(conv_w), const2(conv_b), const2(w_down),
            pl.BlockSpec((1, D), lambda b, i: (0, 0)),
        ],
        out_specs=pl.BlockSpec((None, tm, D), lambda b, i: (b, i, 0)),
        out_shape=jax.ShapeDtypeStruct((B, T, D), F32),
        scratch_shapes=[pltpu.VMEM((tm + 2 * HALO, D), BF16), pltpu.VMEM((2, tm + 2 * HALO, fc), F32),
                        pltpu.VMEM((2, tm + 2 * HALO, fc), F32), pltpu.VMEM((tm, d_ff), BF16)],
        compiler_params=_params("parallel", "parallel"),
        name="conv_ffn",
    )(x, x, x, g.reshape(1, D), mod, mod, mod, w_up, conv_w, conv_b, w_down, final_g.reshape(1, D))


def _swap_halves(w, groups, width):
    lead = w.shape[:-1]
    return jnp.flip(w.reshape(*lead, groups, 2, width // 2), axis=-2).reshape(*lead, groups * width)


def _layout_in_proj(w_in):
    D = w_in.shape[0]
    offs = [0]
    for s in (RET_QK, RET_QK, RET_W, RET_W, MLA_Q_RANK, MLA_KV_RANK, MLA_ROPE, POOL_W):
        offs.append(offs[-1] + s)
    rq, rk, rv, rg, cq, ckv, kr, pool = [w_in[:, offs[j]:offs[j + 1]] for j in range(8)]
    z = lambda n: jnp.zeros((D, n), w_in.dtype)
    place = lambda a: jnp.concatenate([z(MLA_NOPE), a, z(HEAD_PAD - MLA_NOPE - MLA_ROPE)], axis=1)
    cols = [rk, _swap_halves(rk, RET_HEADS, RET_DK), rv, rq, _swap_halves(rq, RET_HEADS, RET_DK), rg,
            cq, z(512 - MLA_Q_RANK), place(kr), place(_swap_halves(kr, 1, MLA_ROPE)), ckv, pool]
    return jnp.concatenate(cols, axis=1).astype(BF16)


def _layout_uq(w_uq):
    r = w_uq.shape[0]
    w = w_uq.reshape(r, MLA_HEADS, MLA_NOPE + MLA_ROPE)
    nope, rope = w[..., :MLA_NOPE], w[..., MLA_NOPE:]
    z = lambda n: jnp.zeros((r, MLA_HEADS, n), w_uq.dtype)
    tail = HEAD_PAD - MLA_NOPE - MLA_ROPE
    per_head = jnp.concatenate([nope, rope, z(tail), z(MLA_NOPE), _swap_halves(rope, 1, MLA_ROPE), z(tail)], axis=-1)
    w = per_head.reshape(r, MLA_HEADS * 2 * HEAD_PAD)
    return jnp.concatenate([w, jnp.zeros((512 - r, w.shape[1]), w.dtype)], axis=0).astype(BF16)


def _layout_ukv(w_ukv):
    r = w_ukv.shape[0]
    w = w_ukv.reshape(r, MLA_HEADS, MLA_NOPE + MLA_DV)
    z = lambda n: jnp.zeros((r, MLA_HEADS, n), w_ukv.dtype)
    per_head = jnp.concatenate([w[..., :MLA_NOPE], z(HEAD_PAD - MLA_NOPE), w[..., MLA_NOPE:], z(HEAD_PAD - MLA_DV)], axis=-1)
    return per_head.reshape(r, MLA_HEADS * 2 * HEAD_PAD).astype(BF16)


def _rope_tables(t):
    rows = t // GRID_W
    row = jnp.repeat(jnp.arange(rows, dtype=F32), GRID_W)
    col = jnp.tile(jnp.arange(GRID_W, dtype=F32), rows)
    n_freq = MLA_ROPE // 4
    inv = ROPE_THETA ** (-jnp.arange(n_freq, dtype=F32) / n_freq)
    ang = jnp.concatenate([row[:, None] * inv, col[:, None] * inv], axis=-1)
    cos, sin = jnp.cos(ang), jnp.sin(ang)
    cc = jnp.concatenate([cos, cos], axis=-1)
    ss = jnp.concatenate([-sin, sin], axis=-1)
    ret_cos, ret_sin = jnp.tile(cc, (1, RET_HEADS)), jnp.tile(ss, (1, RET_HEADS))
    tail = HEAD_PAD - MLA_NOPE - MLA_ROPE
    one, zero = jnp.ones((t, MLA_NOPE), F32), jnp.zeros((t, MLA_NOPE), F32)
    mla_cos = jnp.concatenate([one, cc, jnp.ones((t, tail), F32)], axis=-1)
    mla_sin = jnp.concatenate([zero, ss, jnp.zeros((t, tail), F32)], axis=-1)
    return ret_cos, ret_sin, mla_cos, mla_sin


def kernel(x, c, ctx, c_ctx, w_mod, b_mod, norm1_g, w_in, ret_decay_f, ret_decay_b, mla_q_norm_g, w_uq,
           mla_kv_norm_g, w_ukv, pool_w, pool_scale, w_out, norm2_g, w_up, conv_w, conv_b, w_down,
           final_norm_g):
    B, T, D = x.shape
    L = w_mod.shape[0]
    d_ff = w_down.shape[1]
    assert T % 512 == 0 and ctx.shape[1] % RET_CHUNK == 0 and B + 1 <= SUBLANES

    ret_cos, ret_sin, mla_cos, mla_sin = _rope_tables(T)
    c_rows = jnp.concatenate([c, c_ctx[None, :], jnp.zeros((SUBLANES - B - 1, D), c.dtype)], axis=0)
    mod_all = _modulation(c_rows, w_mod, b_mod)
    lat_row = lambda b: b
    ctx_row = lambda b: B

    xc = ctx
    zero_state = jnp.zeros((B, RET_QK, RET_W), F32)
    for l in range(L):
        with_ctx_out = l < L - 1
        mod = mod_all[l].reshape(SUBLANES * 6, 1, D)
        w_p = _layout_in_proj(w_in[l])
        wq = _layout_uq(w_uq[l])
        wkv = _layout_ukv(w_ukv[l])
        gq = jnp.concatenate([mla_q_norm_g[l], jnp.zeros((512 - MLA_Q_RANK,), F32)]).reshape(1, 512)
        gkv = mla_kv_norm_g[l].reshape(1, MLA_KV_RANK)
        pool_bd = jax.scipy.linalg.block_diag(*[pool_w[l, g] for g in range(len(POOL_WINDOWS))]).astype(BF16)
        wo = w_out[l].astype(BF16)
        wo_r, wo_m, wo_p = wo[:RET_W], wo[RET_W:RET_W + MLA_W], wo[RET_W + MLA_W:]
        ffn_w = (w_up[l].astype(BF16), conv_w[l], conv_b[l].reshape(1, 2 * d_ff), w_down[l].astype(BF16))

        p_lat = _in_projection(x, norm1_g[l], mod, lat_row, w_p)
        p_ctx = _in_projection(xc, norm1_g[l], mod, ctx_row, w_p)

        yr_ctx, s_f, s_b = _retention(p_ctx, ret_cos, ret_sin, ret_decay_f[l], ret_decay_b[l], zero_state, zero_state, False)
        yr_lat, _, _ = _retention(p_lat, ret_cos, ret_sin, ret_decay_f[l], ret_decay_b[l], s_f, s_b, True)

        q_ctx, k_ctx, v_ctx = _mla_projection(p_ctx, gq, gkv, wq, wkv, mla_cos, mla_sin, False)
        q_lat, k_lat, v_lat = _mla_projection(p_lat, gq, gkv, wq, wkv, mla_cos, mla_sin, True)
        ym_lat = _attention(q_lat, k_ctx, v_ctx, k_lat, v_lat)

        yp_lat = _pool(p_lat, pool_bd, pool_scale[l])
        x = _out_projection(x, yr_lat, ym_lat, yp_lat, mod, lat_row, wo_r, wo_m, wo_p)
        x = _ffn(x, norm2_g[l], mod, lat_row, *ffn_w, final_norm_g, not with_ctx_out)
        if with_ctx_out:
            ym_ctx = _attention(q_ctx, k_ctx, v_ctx)
            yp_ctx = _pool(p_ctx, pool_bd, pool_scale[l])
            xc = _out_projection(xc, yr_ctx, ym_ctx, yp_ctx, mod, ctx_row, wo_r, wo_m, wo_p)
            xc = _ffn(xc, norm2_g[l], mod, ctx_row, *ffn_w, final_norm_g, False)
    return x
```

```python
import functools

import jax
import jax.numpy as jnp
from jax import lax
from jax.experimental import pallas as pl
from jax.experimental.pallas import tpu as pltpu

BF16 = jnp.bfloat16
F32 = jnp.float32

GRID_W = 64
ROPE_THETA = 10000.0
NORM_EPS = 1e-6
RET_HEADS = 4
RET_DK = 32
RET_DV = 64
MLA_HEADS = 8
MLA_Q_RANK = 384
MLA_KV_RANK = 256
MLA_NOPE = 64
MLA_ROPE = 32
MLA_DV = 64
POOL_WINDOWS = (2, 4, 8, 16)
POOL_GROUP = 64

LANES = 128
SUBLANES = 8
BF16_ROWS = 16
VMEM_LIMIT_BYTES = 56 * 1024 * 1024

RET_W = RET_HEADS * RET_DV
RET_QK = RET_HEADS * RET_DK
MLA_W = MLA_HEADS * MLA_DV
POOL_W = len(POOL_WINDOWS) * POOL_GROUP
HEAD_PAD = 128
HALO = 16
LOG2_E = 1.4426950408889634
V_ROWS = 256
RET_CHUNK = 128

P_RET = 0
P_CQ = 1024
P_KR = 1536
P_CKV = 1792
P_POOL = 2048
P_WIDTH = 2304


def _params(*semantics):
    return pltpu.CompilerParams(dimension_semantics=semantics, vmem_limit_bytes=VMEM_LIMIT_BYTES)


def _resident(block_shape, index_map):
    return pl.BlockSpec(block_shape, index_map, pipeline_mode=pl.Buffered(1))


def _rms(x, width):
    return x * lax.rsqrt(jnp.sum(x * x, axis=-1, keepdims=True) * (1.0 / width) + NORM_EPS)


def _silu(x):
    return x * jax.nn.sigmoid(x)


def _log_sigmoid(d):
    return jnp.minimum(d, 0.0) - jnp.log1p(jnp.exp(-jnp.abs(d)))


def _mod_kernel(c_ref, w_ref, b_ref, o_ref):
    s = _silu(c_ref[...])
    o_ref[...] = jnp.dot(s.astype(BF16), w_ref[...].astype(BF16), preferred_element_type=F32) + b_ref[...]


def _modulation(c_rows, w_mod, b_mod):
    L, D, W = w_mod.shape
    tn = 1536
    return pl.pallas_call(
        _mod_kernel,
        grid=(L, W // tn),
        in_specs=[
            pl.BlockSpec((SUBLANES, D), lambda l, j: (0, 0)),
            pl.BlockSpec((None, D, tn), lambda l, j: (l, 0, j)),
            pl.BlockSpec((None, 1, tn), lambda l, j: (l, 0, j)),
        ],
        out_specs=pl.BlockSpec((None, SUBLANES, tn), lambda l, j: (l, 0, j)),
        out_shape=jax.ShapeDtypeStruct((L, SUBLANES, W), F32),
        compiler_params=_params("parallel", "parallel"),
        name="modulation",
    )(c_rows, w_mod, b_mod.reshape(L, 1, W))


def _inproj_kernel(x_ref, g_ref, sh_ref, sc_ref, w_ref, o_ref):
    x = x_ref[...]
    h = _rms(x, x.shape[-1]) * g_ref[...]
    h = h * (1.0 + sc_ref[...]) + sh_ref[...]
    o_ref[...] = jnp.dot(h.astype(BF16), w_ref[...], preferred_element_type=F32)


def _mod_index(section, row_of_batch):
    def index(b, *_):
        return (row_of_batch(b) * 6 + section, 0, 0)

    return index


def _in_projection(x, g, mod, row_of_batch, w):
    B, T, D = x.shape
    tm = min(T, 512)
    vec = lambda sec: pl.BlockSpec((None, 1, D), _mod_index(sec, row_of_batch))
    return pl.pallas_call(
        _inproj_kernel,
        grid=(B, T // tm),
        in_specs=[
            pl.BlockSpec((None, tm, D), lambda b, i: (b, i, 0)),
            pl.BlockSpec((1, D), lambda b, i: (0, 0)),
            vec(0),
            vec(1),
            _resident((D, P_WIDTH), lambda b, i: (0, 0)),
        ],
        out_specs=pl.BlockSpec((None, tm, P_WIDTH), lambda b, i: (b, i, 0)),
        out_shape=jax.ShapeDtypeStruct((B, T, P_WIDTH), F32),
        compiler_params=_params("parallel", "parallel"),
        name="in_projection",
    )(x, g.reshape(1, D), mod, mod, w)


def _ret_rope(a, a_swap, cos, sin, rotate):
    if not rotate:
        return a
    return a * cos + a_swap * sin


def _ret_state_kernel(pf_ref, pb_ref, cf_ref, sf_ref, cb_ref, sb_ref, dfl_ref, dbl_ref, dfc_ref, dbc_ref,
                      s0f_ref, s0b_ref, sf_all_ref, sb_all_ref, sf_fin_ref, sb_fin_ref, stf_ref, stb_ref,
                      *, rotate):
    c = RET_CHUNK
    i = pl.program_id(0)

    @pl.when(i == 0)
    def _():
        stf_ref[...] = s0f_ref[...]
        stb_ref[...] = s0b_ref[...]

    sf_all_ref[...] = stf_ref[...]
    sb_all_ref[...] = stb_ref[...]

    n = lax.broadcasted_iota(jnp.int32, (c, RET_QK), 0).astype(F32)
    rows = lax.broadcasted_iota(jnp.int32, (RET_QK, RET_W), 0)
    cols = lax.broadcasted_iota(jnp.int32, (RET_QK, RET_W), 1)
    head_block = (rows // RET_DK) == (cols // RET_DV)

    def update(p_ref, cos_ref, sin_ref, lg_lane, lg_col, pos_pow, st_ref):
        k_decay = jnp.exp(lg_lane * pos_pow) * (RET_DK ** -0.5)
        carry_decay = jnp.exp(lg_col * float(c))
        for b in range(p_ref.shape[0]):
            k = _ret_rope(p_ref[b, :, 0:RET_QK], p_ref[b, :, RET_QK:2 * RET_QK], cos_ref[...], sin_ref[...], rotate)
            v = p_ref[b, :, 2 * RET_QK:2 * RET_QK + RET_W]
            kv = lax.dot_general((k * k_decay).astype(BF16), v.astype(BF16), (((0,), (0,)), ((), ())),
                                 preferred_element_type=F32)
            st_ref[b] = st_ref[b] * carry_decay + jnp.where(head_block, kv, 0.0)

    update(pf_ref, cf_ref, sf_ref, _log_sigmoid(dfl_ref[...]), _log_sigmoid(dfc_ref[...]), (c - 1.0) - n, stf_ref)
    update(pb_ref, cb_ref, sb_ref, _log_sigmoid(dbl_ref[...]), _log_sigmoid(dbc_ref[...]), n, stb_ref)

    @pl.when(i == pl.num_programs(0) - 1)
    def _():
        sf_fin_ref[...] = stf_ref[...]
        sb_fin_ref[...] = stb_ref[...]


def _ret_out_kernel(p_ref, cos_ref, sin_ref, dfl_ref, dbl_ref, dfw_ref, dbw_ref, sf_ref, sb_ref, ones_ref,
                    o_ref, *, rotate, n_sub):
    c = RET_CHUNK
    kr = lax.broadcasted_iota(jnp.int32, (RET_HEADS * c, RET_QK), 0) // c
    kl = lax.broadcasted_iota(jnp.int32, (RET_HEADS * c, RET_QK), 1) // RET_DK
    vr = lax.broadcasted_iota(jnp.int32, (RET_HEADS * c, RET_W), 0) // c
    vl = lax.broadcasted_iota(jnp.int32, (RET_HEADS * c, RET_W), 1) // RET_DV

    lgf_w = _log_sigmoid(dfw_ref[...])
    lgb_w = _log_sigmoid(dbw_ref[...])
    nq = lax.broadcasted_iota(jnp.int32, (c, RET_HEADS * c), 0)
    mk = lax.broadcasted_iota(jnp.int32, (c, RET_HEADS * c), 1) % c
    diff = (nq - mk).astype(F32)
    dmat = jnp.exp(jnp.where(diff >= 0.0, lgf_w, -lgb_w) * diff)
    n = lax.broadcasted_iota(jnp.int32, (c, RET_QK), 0).astype(F32)
    qf_decay = jnp.exp(_log_sigmoid(dfl_ref[...]) * (n + 1.0))
    qb_decay = jnp.exp(_log_sigmoid(dbl_ref[...]) * (float(c) - n))

    for s in range(n_sub):
        r = slice(s * c, (s + 1) * c)
        cos, sin = cos_ref[r, :], sin_ref[r, :]
        k = _ret_rope(p_ref[r, 0:RET_QK], p_ref[r, RET_QK:2 * RET_QK], cos, sin, rotate) * (RET_DK ** -0.5)
        v = p_ref[r, 2 * RET_QK:2 * RET_QK + RET_W]
        q = _ret_rope(p_ref[r, 512:512 + RET_QK], p_ref[r, 512 + RET_QK:512 + 2 * RET_QK], cos, sin, rotate)
        gate = p_ref[r, 768:768 + RET_W]
        k_stack = jnp.where(kr == kl, jnp.tile(k, (RET_HEADS, 1)), 0.0).astype(BF16)
        v_stack = jnp.where(vr == vl, jnp.tile(v, (RET_HEADS, 1)), 0.0).astype(BF16)

        scores = lax.dot_general(q.astype(BF16), k_stack, (((1,), (1,)), ((), ())), preferred_element_type=F32)
        a = (scores * dmat).astype(BF16)
        o = jnp.dot(a, v_stack, preferred_element_type=F32)
        o = o + jnp.dot((q * qf_decay).astype(BF16), sf_ref[s].astype(BF16), preferred_element_type=F32)
        o = o + jnp.dot((q * qb_decay).astype(BF16), sb_ref[s].astype(BF16), preferred_element_type=F32)

        sq = o * o
        hi = sq.astype(BF16)
        lo = (sq - hi.astype(F32)).astype(BF16)
        ssum = (jnp.dot(hi, ones_ref[...], preferred_element_type=F32)
                + jnp.dot(lo, ones_ref[...], preferred_element_type=F32))
        o = o * lax.rsqrt(ssum * (1.0 / RET_DV) + NORM_EPS)
        o_ref[r, :] = (o * _silu(gate)).astype(o_ref.dtype)


def _retention(p, cos, sin, decay_f, decay_b, s0f, s0b, rotate):
    B, T, _ = p.shape
    c = RET_CHUNK
    nc = T // c
    rep = lambda d, r: jnp.repeat(d.astype(F32), r)
    dfl, dbl = rep(decay_f, RET_DK).reshape(1, RET_QK), rep(decay_b, RET_DK).reshape(1, RET_QK)
    dfc, dbc = dfl.reshape(RET_QK, 1), dbl.reshape(RET_QK, 1)
    dfw, dbw = rep(decay_f, c).reshape(1, RET_HEADS * c), rep(decay_b, c).reshape(1, RET_HEADS * c)
    small1 = lambda shape: pl.BlockSpec(shape, lambda i: (0,) * len(shape))
    rope_f = pl.BlockSpec((c, RET_QK), lambda i: (i, 0))
    rope_b = pl.BlockSpec((c, RET_QK), lambda i: (nc - 1 - i, 0))
    state = pl.BlockSpec((B, RET_QK, RET_W), lambda i: (0, 0, 0))

    sf_all, sb_all, sf_fin, sb_fin = pl.pallas_call(
        functools.partial(_ret_state_kernel, rotate=rotate),
        grid=(nc,),
        in_specs=[
            pl.BlockSpec((B, c, 512), lambda i: (0, i, 0)),
            pl.BlockSpec((B, c, 512), lambda i: (0, nc - 1 - i, 0)),
            rope_f, rope_f, rope_b, rope_b,
            small1((1, RET_QK)), small1((1, RET_QK)), small1((RET_QK, 1)), small1((RET_QK, 1)),
            state, state,
        ],
        out_specs=[
            pl.BlockSpec((B, None, RET_QK, RET_W), lambda i: (0, i, 0, 0)),
            pl.BlockSpec((B, None, RET_QK, RET_W), lambda i: (0, nc - 1 - i, 0, 0)),
            state, state,
        ],
        out_shape=[
            jax.ShapeDtypeStruct((B, nc, RET_QK, RET_W), F32),
            jax.ShapeDtypeStruct((B, nc, RET_QK, RET_W), F32),
            jax.ShapeDtypeStruct((B, RET_QK, RET_W), F32),
            jax.ShapeDtypeStruct((B, RET_QK, RET_W), F32),
        ],
        scratch_shapes=[pltpu.VMEM((B, RET_QK, RET_W), F32), pltpu.VMEM((B, RET_QK, RET_W), F32)],
        compiler_params=_params("arbitrary"),
        name="retention_states",
    )(p, p, cos, sin, cos, sin, dfl, dbl, dfc, dbc, s0f, s0b)

    gi = lax.broadcasted_iota(jnp.int32, (RET_W, RET_W), 0) // RET_DV
    gj = lax.broadcasted_iota(jnp.int32, (RET_W, RET_W), 1) // RET_DV
    ones_bd = (gi == gj).astype(BF16)
    n_sub = min(nc, 4)
    tm = n_sub * c
    small = lambda shape: pl.BlockSpec(shape, lambda b, i: (0,) * len(shape))
    rope = pl.BlockSpec((tm, RET_QK), lambda b, i: (i, 0))
    chunk_states = pl.BlockSpec((None, n_sub, RET_QK, RET_W), lambda b, i: (b, i, 0, 0))
    y = pl.pallas_call(
        functools.partial(_ret_out_kernel, rotate=rotate, n_sub=n_sub),
        grid=(B, T // tm),
        in_specs=[
            pl.BlockSpec((None, tm, 1024), lambda b, i: (b, i, 0)),
            rope, rope,
            small((1, RET_QK)), small((1, RET_QK)), small((1, RET_HEADS * c)), small((1, RET_HEADS * c)),
            chunk_states, chunk_states,
            small((RET_W, RET_W)),
        ],
        out_specs=pl.BlockSpec((None, tm, RET_W), lambda b, i: (b, i, 0)),
        out_shape=jax.ShapeDtypeStruct((B, T, RET_W), BF16),
        compiler_params=_params("parallel", "parallel"),
        name="retention_outputs",
    )(p, cos, sin, dfl, dbl, dfw, dbw, sf_all, sb_all, ones_bd)
    return y, sf_fin, sb_fin


def _pool_tile(prev_ref, x_ref, next_ref, w_ref, scale_ref, *, tp, t_total):
    i = pl.program_id(1)
    x = x_ref[...]
    prev = jnp.where(i > 0, prev_ref[...], 0.0)
    nxt = jnp.where(i < pl.num_programs(1) - 1, next_ref[...], 0.0)
    xp = jnp.concatenate([prev, x, nxt], axis=0)
    rows = tp + 2 * HALO
    back = lambda a, s: pltpu.roll(a, s, axis=0)
    ahead = lambda a, s: pltpu.roll(a, rows - s, axis=0)
    a2 = xp + back(xp, 1)
    a4 = a2 + back(a2, 2)
    a8 = a4 + back(a4, 4)
    a16 = a8 + back(a8, 8)
    lane = lax.broadcasted_iota(jnp.int32, (rows, POOL_W), 1)
    group = lane // POOL_GROUP
    win = jnp.where(group == 0, a2, jnp.where(group == 1, ahead(a4, 1), jnp.where(group == 2, ahead(a8, 3), ahead(a16, 7))))
    win = win[HALO:HALO + tp]

    lane = lax.broadcasted_iota(jnp.int32, (tp, POOL_W), 1)
    half = jnp.left_shift(1, lane // POOL_GROUP)
    t = lax.broadcasted_iota(jnp.int32, (tp, POOL_W), 0) + i * tp
    count = jnp.minimum(t + half, t_total) - jnp.maximum(t - half, 0)
    d = win / count.astype(F32) - x
    y = jnp.dot(d.astype(BF16), w_ref[...], preferred_element_type=F32) * scale_ref[...]
    return y.astype(BF16)


def _pool_kernel(prev_ref, x_ref, next_ref, w_ref, scale_ref, o_ref, *, tp, t_total):
    o_ref[...] = _pool_tile(prev_ref, x_ref, next_ref, w_ref, scale_ref, tp=tp, t_total=t_total)


def _pool(p, w_bd, scale):
    B, T, _ = p.shape
    tp = min(T, 1024)
    col = P_POOL // POOL_W
    hb = tp // HALO
    return pl.pallas_call(
        functools.partial(_pool_kernel, tp=tp, t_total=T),
        grid=(B, T // tp),
        in_specs=[
            pl.BlockSpec((None, HALO, POOL_W), lambda b, i: (b, jnp.maximum(i * hb - 1, 0), col)),
            pl.BlockSpec((None, tp, POOL_W), lambda b, i: (b, i, col)),
            pl.BlockSpec((None, HALO, POOL_W), lambda b, i: (b, jnp.minimum((i + 1) * hb, T // HALO - 1), col)),
            pl.BlockSpec((POOL_W, POOL_W), lambda b, i: (0, 0)),
            pl.BlockSpec((1, POOL_W), lambda b, i: (0, 0)),
        ],
        out_specs=pl.BlockSpec((None, tp, POOL_W), lambda b, i: (b, i, 0)),
        out_shape=jax.ShapeDtypeStruct((B, T, POOL_W), BF16),
        compiler_params=_params("parallel", "parallel"),
        name="pooling",
    )(p, p, p, w_bd, scale.reshape(1, POOL_W))


def _mla_proj_kernel(cq_ref, kr_ref, ckv_ref, gq_ref, gkv_ref, wq_ref, wkv_ref, cos_ref, sin_ref,
                     q_ref, k_ref, v_ref, *, rotate):
    scale = (MLA_NOPE + MLA_ROPE) ** -0.5 * LOG2_E
    cq = _rms(cq_ref[...], MLA_Q_RANK) * gq_ref[...]
    q = jnp.dot(cq.astype(BF16), wq_ref[...], preferred_element_type=F32)
    ckv = _rms(ckv_ref[...], MLA_KV_RANK) * gkv_ref[...]
    kv = jnp.dot(ckv.astype(BF16), wkv_ref[...], preferred_element_type=F32)
    kr = kr_ref[:, 0:HEAD_PAD]
    if rotate:
        kr = kr * cos_ref[...] + kr_ref[:, HEAD_PAD:2 * HEAD_PAD] * sin_ref[...]
    ones_col = (lax.broadcasted_iota(jnp.int32, (1, HEAD_PAD), 1) == MLA_DV).astype(F32)
    zeros = jnp.zeros((V_ROWS, HEAD_PAD), BF16)
    for h in range(MLA_HEADS):
        base = 2 * HEAD_PAD * h
        qh = q[:, base:base + HEAD_PAD]
        if rotate:
            qh = qh * cos_ref[...] + q[:, base + HEAD_PAD:base + 2 * HEAD_PAD] * sin_ref[...]
        q_ref[h] = (qh * scale).astype(BF16)
        k_ref[h] = (kv[:, base:base + HEAD_PAD] + kr).astype(BF16)
        vh = (kv[:, base + HEAD_PAD:base + 2 * HEAD_PAD] + ones_col).astype(BF16)
        for r in range(vh.shape[0] // V_ROWS):
            rows = slice(r * V_ROWS, (r + 1) * V_ROWS)
            halves = [vh[rows], zeros] if r % 2 == 0 else [zeros, vh[rows]]
            v_ref[h, rows, :] = jnp.concatenate(halves, axis=-1)


def _mla_projection(p, gq, gkv, wq, wkv, cos, sin, rotate):
    B, T, _ = p.shape
    tm = min(T, 2 * V_ROWS)
    heads = pl.BlockSpec((None, MLA_HEADS, tm, HEAD_PAD), lambda b, i: (b, 0, i, 0))
    shape = jax.ShapeDtypeStruct((B, MLA_HEADS, T, HEAD_PAD), BF16)
    v_heads = pl.BlockSpec((None, MLA_HEADS, tm, 2 * HEAD_PAD), lambda b, i: (b, 0, i, 0))
    v_shape = jax.ShapeDtypeStruct((B, MLA_HEADS, T, 2 * HEAD_PAD), BF16)
    const = lambda shape_: _resident(shape_, lambda b, i: (0, 0))
    return pl.pallas_call(
        functools.partial(_mla_proj_kernel, rotate=rotate),
        grid=(B, T // tm),
        in_specs=[
            pl.BlockSpec((None, tm, 512), lambda b, i: (b, i, P_CQ // 512)),
            pl.BlockSpec((None, tm, 256), lambda b, i: (b, i, P_KR // 256)),
            pl.BlockSpec((None, tm, 256), lambda b, i: (b, i, P_CKV // 256)),
            const((1, 512)), const((1, 256)), const((512, 2 * HEAD_PAD * MLA_HEADS)),
            const((256, 2 * HEAD_PAD * MLA_HEADS)),
            pl.BlockSpec((tm, HEAD_PAD), lambda b, i: (i, 0)),
            pl.BlockSpec((tm, HEAD_PAD), lambda b, i: (i, 0)),
        ],
        out_specs=[heads, heads, v_heads],
        out_shape=[shape, shape, v_shape],
        compiler_params=_params("parallel", "parallel"),
        name="mla_projection",
    )(p, p, p, gq, gkv, wq, wkv, cos, sin)


def _attn_kernel(q_ref, kc_ref, vc_ref, *rest, tk, n_lat, heads):
    if n_lat:
        kl_ref, vl_ref, o_ref, m_ref, acc_ref, s_ref = rest
    else:
        o_ref, m_ref, acc_ref = rest
    m_ref[...] = jnp.full_like(m_ref, -1e30)
    acc_ref[...] = jnp.zeros_like(acc_ref)

    def scores(h, k):
        return lax.dot_general(q_ref[h], k, (((1,), (1,)), ((), ())), preferred_element_type=F32)

    def update(h, s, v):
        m_old = m_ref[h]
        m_new = jnp.maximum(m_old[:, 0:1], jnp.max(s, axis=-1, keepdims=True))
        p = jnp.exp2(s - m_new)
        alpha = jnp.exp2(m_old - m_new)
        pv = jnp.dot(p.astype(BF16), v, preferred_element_type=F32)
        acc_ref[h] = jnp.concatenate([alpha, alpha], axis=-1) * acc_ref[h] + pv
        m_ref[h] = jnp.broadcast_to(m_new, m_old.shape)

    def lat(ref, h, j):
        return ref[h, pl.ds(pl.multiple_of(j * tk, tk), tk), :]

    def half(j, cur, nxt):
        for h in range(heads):
            if nxt is not None:
                s_ref[nxt, h] = scores(h, lat(kl_ref, h, j + 1))
            update(h, s_ref[cur, h], lat(vl_ref, h, j))

    ctx_scores = [scores(h, kc_ref[h]) for h in range(heads)]
    if n_lat:
        for h in range(heads):
            s_ref[0, h] = scores(h, lat(kl_ref, h, 0))
    for h in range(heads):
        update(h, ctx_scores[h], vc_ref[h])
    if n_lat:
        def body(jj, carry):
            half(2 * jj, 0, 1)
            half(2 * jj + 1, 1, 0)
            return carry

        lax.fori_loop(0, n_lat // 2 - 1, body, 0)
        half(n_lat - 2, 0, 1)
        half(n_lat - 1, 1, None)
    outs = []
    for h in range(heads):
        acc = acc_ref[h][:, 0:HEAD_PAD] + acc_ref[h][:, HEAD_PAD:2 * HEAD_PAD]
        outs.append(acc[:, 0:MLA_DV] / acc[:, MLA_DV:MLA_DV + 1])
    o_ref[...] = jnp.concatenate(outs, axis=-1).astype(o_ref.dtype)


def _attention(q, k_ctx, v_ctx, k_lat=None, v_lat=None):
    B, H, Tq, _ = q.shape
    Tc = k_ctx.shape[2]
    heads = 2
    tq = min(Tq, 1024)
    tk = 1024
    n_lat = 0 if k_lat is None else k_lat.shape[2] // tk
    assert tk % (2 * V_ROWS) == 0 and Tc % V_ROWS == 0
    whole = lambda t, w: pl.BlockSpec((None, heads, t, w), lambda b, g, i: (b, g, 0, 0))
    in_specs = [pl.BlockSpec((None, heads, tq, HEAD_PAD), lambda b, g, i: (b, g, i, 0)),
                whole(Tc, HEAD_PAD), whole(Tc, 2 * HEAD_PAD)]
    args = [q, k_ctx, v_ctx]
    scratch = [pltpu.VMEM((heads, tq, HEAD_PAD), F32), pltpu.VMEM((heads, tq, 2 * HEAD_PAD), F32)]
    if n_lat:
        assert n_lat % 2 == 0
        in_specs += [whole(k_lat.shape[2], HEAD_PAD), whole(k_lat.shape[2], 2 * HEAD_PAD)]
        args += [k_lat, v_lat]
        scratch.append(pltpu.VMEM((2, heads, tq, tk), F32))
    return pl.pallas_call(
        functools.partial(_attn_kernel, tk=tk, n_lat=n_lat, heads=heads),
        grid=(B, H // heads, Tq // tq),
        in_specs=in_specs,
        out_specs=pl.BlockSpec((None, tq, heads * MLA_DV), lambda b, g, i: (b, i, g)),
        out_shape=jax.ShapeDtypeStruct((B, Tq, H * MLA_DV), BF16),
        scratch_shapes=scratch,
        compiler_params=_params("parallel", "parallel", "parallel"),
        name="attention",
    )(*args)


def _outproj_kernel(x_ref, yr_ref, ym_ref, pprev_ref, px_ref, pnext_ref, pw_ref, ps_ref, gate_ref, wr_ref, wm_ref,
                    wp_ref, o_ref, *, tm, t_total):
    yp = _pool_tile(pprev_ref, px_ref, pnext_ref, pw_ref, ps_ref, tp=tm, t_total=t_total)
    y = jnp.dot(yr_ref[...], wr_ref[...], preferred_element_type=F32)
    y = y + jnp.dot(ym_ref[...], wm_ref[...], preferred_element_type=F32)
    y = y + jnp.dot(yp, wp_ref[...], preferred_element_type=F32)
    o_ref[...] = x_ref[...] + gate_ref[...] * y


def _out_projection(x, y_ret, y_mla, p, pool_bd, pool_scale, mod, row_of_batch, w_r, w_m, w_p):
    B, T, D = x.shape
    tm = min(T, 512)
    col = P_POOL // POOL_W
    hb = tm // HALO
    rows = lambda w: pl.BlockSpec((None, tm, w), lambda b, i: (b, i, 0))
    const = lambda shape: _resident(shape, lambda b, i: (0, 0))
    return pl.pallas_call(
        functools.partial(_outproj_kernel, tm=tm, t_total=T),
        grid=(B, T // tm),
        in_specs=[
            rows(D), rows(RET_W), rows(MLA_W),
            pl.BlockSpec((None, HALO, POOL_W), lambda b, i: (b, jnp.maximum(i * hb - 1, 0), col)),
            pl.BlockSpec((None, tm, POOL_W), lambda b, i: (b, i, col)),
            pl.BlockSpec((None, HALO, POOL_W), lambda b, i: (b, jnp.minimum((i + 1) * hb, T // HALO - 1), col)),
            const((POOL_W, POOL_W)), const((1, POOL_W)),
            pl.BlockSpec((None, 1, D), _mod_index(2, row_of_batch)),
            const((RET_W, D)), const((MLA_W, D)), const((POOL_W, D)),
        ],
        out_specs=rows(D),
        out_shape=jax.ShapeDtypeStruct((B, T, D), F32),
        compiler_params=_params("parallel", "parallel"),
        name="out_projection",
    )(x, y_ret, y_mla, p, p, p, pool_bd, pool_scale.reshape(1, POOL_W), mod, w_r, w_m, w_p)


def _ffn_kernel(xp_ref, x_ref, xn_ref, g_ref, sh_ref, sc_ref, gate_ref, wup_ref, cw_ref, cb_ref, wd_ref, fg_ref,
                o_ref, h_ref, ua_ref, ub_ref, act_ref, *, tm, fc, final):
    i = pl.program_id(1)
    d = x_ref.shape[-1]
    d_ff = wd_ref.shape[0]
    n_chunks = d_ff // fc

    def norm_mod(x):
        h = _rms(x, d) * g_ref[...]
        return h * (1.0 + sc_ref[...]) + sh_ref[...]

    h_ref[0:HALO, :] = jnp.where(i > 0, norm_mod(xp_ref[...]), 0.0).astype(BF16)
    h_ref[HALO:HALO + tm, :] = norm_mod(x_ref[...]).astype(BF16)
    h_ref[HALO + tm:2 * HALO + tm, :] = jnp.where(i < pl.num_programs(1) - 1, norm_mod(xn_ref[...]), 0.0).astype(BF16)
    rows = tm + 2 * HALO

    def up(u_ref, col, slot):
        u_ref[slot] = jnp.dot(h_ref[...], wup_ref[:, col:col + fc], preferred_element_type=F32)

    def conv(u_ref, slot, col):
        cw = cw_ref[:, col:col + fc]
        u = u_ref[slot]
        y = pltpu.roll(u, 1, axis=0) * cw[0:1] + u * cw[1:2] + pltpu.roll(u, rows - 1, axis=0) * cw[2:3]
        return y[HALO:HALO + tm] + cb_ref[:, col:col + fc]

    up(ua_ref, 0, 0)
    up(ub_ref, d_ff, 0)
    for f in range(n_chunks):
        cur, more = f % 2, f + 1 < n_chunks
        if more:
            up(ua_ref, (f + 1) * fc, 1 - cur)
        a = _silu(conv(ua_ref, cur, f * fc))
        if more:
            up(ub_ref, d_ff + (f + 1) * fc, 1 - cur)
        b = conv(ub_ref, cur, d_ff + f * fc)
        act_ref[:, f * fc:(f + 1) * fc] = (a * b).astype(BF16)
    acc = jnp.dot(act_ref[...], wd_ref[...], preferred_element_type=F32)
    y = x_ref[...] + gate_ref[...] * acc
    if final:
        y = _rms(y, d) * fg_ref[...]
    o_ref[...] = y


def _ffn(x, g, mod, row_of_batch, w_up, conv_w, conv_b, w_down, final_g, final):
    B, T, D = x.shape
    d_ff = w_down.shape[0]
    fc = 256
    assert d_ff % fc == 0
    tm = min(T, 512)
    hb = tm // HALO
    vec = lambda sec: pl.BlockSpec((None, 1, D), _mod_index(sec, row_of_batch))
    const2 = lambda a: _resident(a.shape, lambda b, i: (0, 0))
    return pl.pallas_call(
        functools.partial(_ffn_kernel, tm=tm, fc=fc, final=final),
        grid=(B, T // tm),
        in_specs=[
            pl.BlockSpec((None, HALO, D), lambda b, i: (b, jnp.maximum(i * hb - 1, 0), 0)),
            pl.BlockSpec((None, tm, D), lambda b, i: (b, i, 0)),
            pl.BlockSpec((None, HALO, D), lambda b, i: (b, jnp.minimum((i + 1) * hb, T // HALO - 1), 0)),
            pl.BlockSpec((1, D), lambda b, i: (0, 0)),
            vec(3), vec(4), vec(5),
            const2(w_up), const2(conv_w), const2(conv_b), const2(w_down),
            pl.BlockSpec((1, D), lambda b, i: (0, 0)),
        ],
        out_specs=pl.BlockSpec((None, tm, D), lambda b, i: (b, i, 0)),
        out_shape=jax.ShapeDtypeStruct((B, T, D), F32),
        scratch_shapes=[pltpu.VMEM((tm + 2 * HALO, D), BF16), pltpu.VMEM((2, tm + 2 * HALO, fc), F32),
                        pltpu.VMEM((2, tm + 2 * HALO, fc), F32), pltpu.VMEM((tm, d_ff), BF16)],
        compiler_params=_params("parallel", "parallel"),
        name="conv_ffn",
    )(x, x, x, g.reshape(1, D), mod, mod, mod, w_up, conv_w, conv_b, w_down, final_g.reshape(1, D))


def _swap_halves(w, groups, width):
    lead = w.shape[:-1]
    return jnp.flip(w.reshape(*lead, groups, 2, width // 2), axis=-2).reshape(*lead, groups * width)


def _layout_in_proj(w_in):
    D = w_in.shape[0]
    offs = [0]
    for s in (RET_QK, RET_QK, RET_W, RET_W, MLA_Q_RANK, MLA_KV_RANK, MLA_ROPE, POOL_W):
        offs.append(offs[-1] + s)
    rq, rk, rv, rg, cq, ckv, kr, pool = [w_in[:, offs[j]:offs[j + 1]] for j in range(8)]
    z = lambda n: jnp.zeros((D, n), w_in.dtype)
    place = lambda a: jnp.concatenate([z(MLA_NOPE), a, z(HEAD_PAD - MLA_NOPE - MLA_ROPE)], axis=1)
    cols = [rk, _swap_halves(rk, RET_HEADS, RET_DK), rv, rq, _swap_halves(rq, RET_HEADS, RET_DK), rg,
            cq, z(512 - MLA_Q_RANK), place(kr), place(_swap_halves(kr, 1, MLA_ROPE)), ckv, pool]
    return jnp.concatenate(cols, axis=1).astype(BF16)


def _layout_uq(w_uq):
    r = w_uq.shape[0]
    w = w_uq.reshape(r, MLA_HEADS, MLA_NOPE + MLA_ROPE)
    nope, rope = w[..., :MLA_NOPE], w[..., MLA_NOPE:]
    z = lambda n: jnp.zeros((r, MLA_HEADS, n), w_uq.dtype)
    tail = HEAD_PAD - MLA_NOPE - MLA_ROPE
    per_head = jnp.concatenate([nope, rope, z(tail), z(MLA_NOPE), _swap_halves(rope, 1, MLA_ROPE), z(tail)], axis=-1)
    w = per_head.reshape(r, MLA_HEADS * 2 * HEAD_PAD)
    return jnp.concatenate([w, jnp.zeros((512 - r, w.shape[1]), w.dtype)], axis=0).astype(BF16)


def _layout_ukv(w_ukv):
    r = w_ukv.shape[0]
    w = w_ukv.reshape(r, MLA_HEADS, MLA_NOPE + MLA_DV)
    z = lambda n: jnp.zeros((r, MLA_HEADS, n), w_ukv.dtype)
    per_head = jnp.concatenate([w[..., :MLA_NOPE], z(HEAD_PAD - MLA_NOPE), w[..., MLA_NOPE:], z(HEAD_PAD - MLA_DV)], axis=-1)
    return per_head.reshape(r, MLA_HEADS * 2 * HEAD_PAD).astype(BF16)


def _rope_tables(t):
    rows = t // GRID_W
    row = jnp.repeat(jnp.arange(rows, dtype=F32), GRID_W)
    col = jnp.tile(jnp.arange(GRID_W, dtype=F32), rows)
    n_freq = MLA_ROPE // 4
    inv = ROPE_THETA ** (-jnp.arange(n_freq, dtype=F32) / n_freq)
    ang = jnp.concatenate([row[:, None] * inv, col[:, None] * inv], axis=-1)
    cos, sin = jnp.cos(ang), jnp.sin(ang)
    cc = jnp.concatenate([cos, cos], axis=-1)
    ss = jnp.concatenate([-sin, sin], axis=-1)
    ret_cos, ret_sin = jnp.tile(cc, (1, RET_HEADS)), jnp.tile(ss, (1, RET_HEADS))
    tail = HEAD_PAD - MLA_NOPE - MLA_ROPE
    one, zero = jnp.ones((t, MLA_NOPE), F32), jnp.zeros((t, MLA_NOPE), F32)
    mla_cos = jnp.concatenate([one, cc, jnp.ones((t, tail), F32)], axis=-1)
    mla_sin = jnp.concatenate([zero, ss, jnp.zeros((t, tail), F32)], axis=-1)
    return ret_cos, ret_sin, mla_cos, mla_sin


def kernel(x, c, ctx, c_ctx, w_mod, b_mod, norm1_g, w_in, ret_decay_f, ret_decay_b, mla_q_norm_g, w_uq,
           mla_kv_norm_g, w_ukv, pool_w, pool_scale, w_out, norm2_g, w_up, conv_w, conv_b, w_down,
           final_norm_g):
    B, T, D = x.shape
    L = w_mod.shape[0]
    d_ff = w_down.shape[1]
    assert T % 512 == 0 and ctx.shape[1] % RET_CHUNK == 0 and B + 1 <= SUBLANES

    ret_cos, ret_sin, mla_cos, mla_sin = _rope_tables(T)
    c_rows = jnp.concatenate([c, c_ctx[None, :], jnp.zeros((SUBLANES - B - 1, D), c.dtype)], axis=0)
    mod_all = _modulation(c_rows, w_mod, b_mod)
    lat_row = lambda b: b
    ctx_row = lambda b: B

    xc = ctx
    zero_state = jnp.zeros((B, RET_QK, RET_W), F32)
    for l in range(L):
        with_ctx_out = l < L - 1
        mod = mod_all[l].reshape(SUBLANES * 6, 1, D)
        w_p = _layout_in_proj(w_in[l])
        wq = _layout_uq(w_uq[l])
        wkv = _layout_ukv(w_ukv[l])
        gq = jnp.concatenate([mla_q_norm_g[l], jnp.zeros((512 - MLA_Q_RANK,), F32)]).reshape(1, 512)
        gkv = mla_kv_norm_g[l].reshape(1, MLA_KV_RANK)
        pool_bd = jax.scipy.linalg.block_diag(*[pool_w[l, g] for g in range(len(POOL_WINDOWS))]).astype(BF16)
        wo = w_out[l].astype(BF16)
        wo_r, wo_m, wo_p = wo[:RET_W], wo[RET_W:RET_W + MLA_W], wo[RET_W + MLA_W:]
        ffn_w = (w_up[l].astype(BF16), conv_w[l], conv_b[l].reshape(1, 2 * d_ff), w_down[l].astype(BF16))

        p_lat = _in_projection(x, norm1_g[l], mod, lat_row, w_p)
        p_ctx = _in_projection(xc, norm1_g[l], mod, ctx_row, w_p)

        yr_ctx, s_f, s_b = _retention(p_ctx, ret_cos, ret_sin, ret_decay_f[l], ret_decay_b[l], zero_state, zero_state, False)
        yr_lat, _, _ = _retention(p_lat, ret_cos, ret_sin, ret_decay_f[l], ret_decay_b[l], s_f, s_b, True)

        q_ctx, k_ctx, v_ctx = _mla_projection(p_ctx, gq, gkv, wq, wkv, mla_cos, mla_sin, False)
        q_lat, k_lat, v_lat = _mla_projection(p_lat, gq, gkv, wq, wkv, mla_cos, mla_sin, True)
        ym_lat = _attention(q_lat, k_ctx, v_ctx, k_lat, v_lat)

        x = _out_projection(x, yr_lat, ym_lat, p_lat, pool_bd, pool_scale[l], mod, lat_row, wo_r, wo_m, wo_p)
        x = _ffn(x, norm2_g[l], mod, lat_row, *ffn_w, final_norm_g, not with_ctx_out)
        if with_ctx_out:
            ym_ctx = _attention(q_ctx, k_ctx, v_ctx)
            xc = _out_projection(xc, yr_ctx, ym_ctx, p_ctx, pool_bd, pool_scale[l], mod, ctx_row, wo_r, wo_m, wo_p)
            xc = _ffn(xc, norm2_g[l], mod, ctx_row, *ffn_w, final_norm_g, False)
    return x
```
